```python
import math
import jax
import jax.numpy as jnp
from jax import lax
import numpy as np

D_MODEL = 1024
BATCH = 16
SEQ = 2048
DEPTH = 4

GRID_W = 64
CTX_LEN = 256
EPS = 1e-6
N_BRANCH = 4

A_HEADS = 4
A_DK = 128
A_DV = 128
A_WIDTH = A_HEADS * A_DK
A_CHUNK = 64

B_HEADS = 8
B_HEADDIM = 64
B_INNER = B_HEADS * B_HEADDIM
B_GROUPS = 2
B_STATE = 128
B_CONV = 5
B_CONV_DIM = B_INNER + 2 * B_GROUPS * B_STATE
B_CHUNK = 64
ROPE_THETA = 10000.0

C_HEADS = 8
C_HEADDIM = 64
C_WIDTH = C_HEADS * C_HEADDIM
NA_KH = 8
NA_KW = 16

S5_GROUP = 16
S5_GROUPS = 32
S5_WIDTH = S5_GROUP * S5_GROUPS
S5_STATE = 64

N_EXPERTS = 32
TOP_K = 4
D_EXPERT = D_MODEL
SWIGLU_LIMIT = 7.0
SWIGLU_ALPHA = 1.702
MOE_BLOCK = 256

IN_SIZES = (A_WIDTH, A_WIDTH, A_WIDTH, A_WIDTH, A_WIDTH, B_INNER, B_CONV_DIM, 2 * B_HEADS, 3 * C_WIDTH, S5_WIDTH, N_BRANCH * D_MODEL)
D_IN = sum(IN_SIZES)

F32 = jnp.float32

kernel_name = 'hybrid_flow_backbone_block'


def rmsnorm(x, g):
    x32 = x.astype(F32)
    y = x32 * lax.rsqrt(jnp.mean(x32 * x32, axis=-1, keepdims=True) + EPS)
    return (y * g.astype(F32)).astype(x.dtype)


def modulate(h, shift, scale):
    return h * (1.0 + scale[:, None]) + shift[:, None]


def split_cols(z):
    cuts = [int(v) for v in np.cumsum(IN_SIZES)[:-1]]
    return jnp.split(z, cuts, axis=-1)


def dwconv_centred(u, w, b):
    ch = u.shape[-1]
    k = w.shape[0]
    out = lax.conv_general_dilated(u, w[:, None, :].astype(u.dtype), window_strides=(1,), padding=[(k // 2, k // 2)], dimension_numbers=('NWC', 'WIO', 'NWC'), feature_group_count=ch)
    return jax.nn.silu(out + b)


def axial_rope(u, row, col):
    half = u.shape[-1] // 2
    inv_freq = ROPE_THETA ** (-jnp.arange(0, half, 2, dtype=F32) / half)

    def rotate(v, pos):
        ang = pos.astype(F32)[:, None] * inv_freq[None, :]
        cos = jnp.cos(ang)[None, :, None, :]
        sin = jnp.sin(ang)[None, :, None, :]
        v1, v2 = jnp.split(v.astype(F32), 2, axis=-1)
        return jnp.concatenate([v1 * cos - v2 * sin, v1 * sin + v2 * cos], axis=-1)

    ur, uc = jnp.split(u, 2, axis=-1)
    return jnp.concatenate([rotate(ur, row), rotate(uc, col)], axis=-1)


def gla_chunked(q, k, v, log_f, s0):
    bsz, t_len, h, dk = q.shape
    dv = v.shape[-1]
    n = t_len // A_CHUNK

    def chunks(a):
        return jnp.moveaxis(a.reshape(bsz, n, A_CHUNK, h, a.shape[-1]), 1, 0)

    causal = jnp.tril(jnp.ones((A_CHUNK, A_CHUNK), dtype=bool))[:, :, None, None]

    def step(s, inp):
        qc, kc, vc, lfc = inp
        b = jnp.cumsum(lfc, axis=1)
        b_last = b[:, -1]
        diff = b[:, :, None] - b[:, None]
        decay = jnp.exp(jnp.where(causal, diff, -jnp.inf))
        scores = jnp.einsum('btshd,bthd,bshd->bhts', decay, qc, kc)
        o = jnp.einsum('bhts,bshe->bthe', scores, vc) + jnp.einsum('bthd,bhde->bthe', qc * jnp.exp(b), s)
        s_new = jnp.exp(b_last)[..., None] * s + jnp.einsum('bshd,bshe->bhde', kc * jnp.exp(b_last[:, None] - b), vc)
        return s_new, o

    s_fin, o = lax.scan(step, s0, (chunks(q), chunks(k), chunks(v), chunks(log_f)))
    return jnp.moveaxis(o, 0, 1).reshape(bsz, t_len, h, dv), s_fin


def hgrn2_mixer(q_x, ff_x, fb_x, i_x, g_x, q_c, ff_c, fb_c, i_c, g_c, lb, norm_g, need_ctx):
    bsz = q_x.shape[0]
    lbh = lb.reshape(A_HEADS, A_DK)
    log_lb = jnp.log(lbh)
    log_1mlb = jnp.log1p(-lbh)
    scale = A_DK ** -0.5

    def heads(v):
        return v.astype(F32).reshape(v.shape[0], v.shape[1], A_HEADS, -1)

    def forget(fz):
        fz = heads(fz)
        return jnp.logaddexp(log_lb, log_1mlb + jax.nn.log_sigmoid(fz)), (1.0 - lbh) * jax.nn.sigmoid(-fz)

    def fl(v):
        return jnp.flip(v, axis=1)

    s0 = jnp.zeros((bsz, A_HEADS, A_DK, A_DV), F32)
    qx, vx = heads(q_x) * scale, heads(i_x)
    qc, vc = heads(q_c) * scale, heads(i_c)
    lff_x, kf_x = forget(ff_x)
    lfb_x, kb_x = forget(fb_x)
    lff_c, kf_c = forget(ff_c)
    lfb_c, kb_c = forget(fb_c)
    o_cf, s_cf = gla_chunked(qc, kf_c, vc, lff_c, s0)
    o_xf, _ = gla_chunked(qx, kf_x, vx, lff_x, s_cf)
    o_cb, s_cb = gla_chunked(fl(qc), fl(kb_c), fl(vc), fl(lfb_c), s0)
    o_xb, _ = gla_chunked(fl(qx), fl(kb_x), fl(vx), fl(lfb_x), s_cb)

    def finish(o, g):
        o = o * lax.rsqrt(jnp.mean(o * o, axis=-1, keepdims=True) + EPS) * norm_g.astype(F32)
        return o.reshape(g.shape) * jax.nn.silu(g.astype(F32))

    out_x = finish(o_xf + fl(o_xb), g_x)
    out_c = finish(o_cf + fl(o_cb), g_c) if need_ctx else None
    return out_x, out_c


def ssd_chunked(xh, dt, a, bm, cm, s0):
    bsz, t_len, h, p = xh.shape
    g = bm.shape[2]
    nc = t_len // B_CHUNK
    bh = jnp.repeat(bm, h // g, axis=2)
    ch = jnp.repeat(cm, h // g, axis=2)
    xdt = xh * dt[..., None]
    adt = dt * a

    def chunks(u):
        return u.reshape(bsz, nc, B_CHUNK, *u.shape[2:])

    xdt, adt, bh, ch = chunks(xdt), chunks(adt), chunks(bh), chunks(ch)
    a_cum = jnp.cumsum(adt, axis=2)
    causal = jnp.tril(jnp.ones((B_CHUNK, B_CHUNK), dtype=bool))[:, :, None]
    seg = a_cum[:, :, :, None] - a_cum[:, :, None]
    lmat = jnp.exp(jnp.where(causal, seg, -jnp.inf))
    y_diag = jnp.einsum('bctsh,bcthn,bcshn,bcshp->bcthp', lmat, ch, bh, xdt)
    decay_states = jnp.exp(a_cum[:, :, -1:] - a_cum)
    states = jnp.einsum('bcshn,bcsh,bcshp->bchpn', bh, decay_states, xdt)
    chunk_decay = jnp.exp(a_cum[:, :, -1])

    def step(s, inp):
        dec, st = inp
        return dec[..., None, None] * s + st, s

    s_fin, s_starts = lax.scan(step, s0, (jnp.moveaxis(chunk_decay, 1, 0), jnp.moveaxis(states, 1, 0)))
    s_starts = jnp.moveaxis(s_starts, 0, 1)
    y_off = jnp.einsum('bcthn,bchpn,bcth->bcthp', ch, s_starts, jnp.exp(a_cum))
    return (y_diag + y_off).reshape(bsz, t_len, h, p), s_fin


def mamba2_mixer(z_x, xbc_x, dt_x, z_c, xbc_c, dt_c, row, col, conv_w, conv_b, a_log, dt_bias, d_skip, norm_g, need_ctx):
    a = -jnp.exp(a_log.astype(F32))

    def prep(xbc, dt, rotary):
        u = dwconv_centred(xbc, conv_w, conv_b).astype(F32)
        bsz, t_len, _ = u.shape
        xs, bm, cm = jnp.split(u, [B_INNER, B_INNER + B_GROUPS * B_STATE], axis=-1)
        xs = xs.reshape(bsz, t_len, B_HEADS, B_HEADDIM)
        bm = bm.reshape(bsz, t_len, B_GROUPS, B_STATE)
        cm = cm.reshape(bsz, t_len, B_GROUPS, B_STATE)
        if rotary:
            bm, cm = axial_rope(bm, row, col), axial_rope(cm, row, col)
        dt = jax.nn.softplus(dt.astype(F32).reshape(bsz, t_len, 2, B_HEADS) + dt_bias.astype(F32))
        return xs, bm, cm, dt[:, :, 0], dt[:, :, 1]

    def fl(v):
        return jnp.flip(v, axis=1)

    xs_c, b_c, c_c, dtf_c, dtb_c = prep(xbc_c, dt_c, False)
    xs_x, b_x, c_x, dtf_x, dtb_x = prep(xbc_x, dt_x, True)
    s0 = jnp.zeros((xs_x.shape[0], B_HEADS, B_HEADDIM, B_STATE), F32)
    y_cf, s_cf = ssd_chunked(xs_c, dtf_c, a[0], b_c, c_c, s0)
    y_xf, _ = ssd_chunked(xs_x, dtf_x, a[0], b_x, c_x, s_cf)
    y_cb, s_cb = ssd_chunked(fl(xs_c), fl(dtb_c), a[1], fl(b_c), fl(c_c), s0)
    y_xb, _ = ssd_chunked(fl(xs_x), fl(dtb_x), a[1], fl(b_x), fl(c_x), s_cb)

    def finish(xs, yf, yb, z):
        y = yf + fl(yb) + d_skip.astype(F32)[:, None] * xs
        y = y.reshape(z.shape)
        return rmsnorm(y * jax.nn.silu(z.astype(F32)), norm_g)

    out_x = finish(xs_x, y_xf, y_xb, z_x)
    out_c = finish(xs_c, y_cf, y_cb, z_c) if need_ctx else None
    return out_x, out_c


def na_mixer(qkv_x, qkv_c, rpb, need_ctx):
    bsz, t_len, _ = qkv_x.shape
    rows = t_len // GRID_W
    kh = min(NA_KH, rows)
    scale = C_HEADDIM ** -0.5

    def grid(v):
        return v.reshape(bsz, rows, GRID_W, C_HEADS, C_HEADDIM).transpose(0, 3, 1, 2, 4)

    q, k, v = [grid(t) for t in jnp.split(qkv_x, 3, axis=-1)]
    qc, kc, vc = [t.reshape(bsz, -1, C_HEADS, C_HEADDIM) for t in jnp.split(qkv_c, 3, axis=-1)]
    cols = jnp.arange(GRID_W)
    col_start = jnp.clip(cols - NA_KW // 2, 0, GRID_W - NA_KW)
    in_win = (cols[None, :] >= col_start[:, None]) & (cols[None, :] < col_start[:, None] + NA_KW)
    dc = jnp.clip(cols[None, :] - cols[:, None], -(NA_KW - 1), NA_KW - 1) + NA_KW - 1

    def row_block(args):
        r, q_r = args
        start = jnp.clip(r - kh // 2, 0, rows - kh)
        k_r = lax.dynamic_slice_in_dim(k, start, kh, axis=2)
        v_r = lax.dynamic_slice_in_dim(v, start, kh, axis=2)
        dr = start + jnp.arange(kh) - r + NA_KH - 1
        bias = rpb[:, dr[None, :, None], dc[:, None, :]]
        s_nb = jnp.einsum('bhqd,bhkwd->bhqkw', q_r, k_r).astype(F32) * scale + bias.astype(F32)
        s_nb = jnp.where(in_win[:, None, :], s_nb, -jnp.inf)
        s_cx = jnp.einsum('bhqd,bchd->bhqc', q_r, kc).astype(F32) * scale
        p = jax.nn.softmax(jnp.concatenate([s_nb.reshape(bsz, C_HEADS, GRID_W, kh * GRID_W), s_cx], axis=-1), axis=-1)
        p_nb = p[..., :kh * GRID_W].reshape(bsz, C_HEADS, GRID_W, kh, GRID_W)
        return jnp.einsum('bhqkw,bhkwd->bhqd', p_nb, v_r.astype(F32)) + jnp.einsum('bhqc,bchd->bhqd', p[..., kh * GRID_W:], vc.astype(F32))

    o = lax.map(row_block, (jnp.arange(rows), jnp.moveaxis(q, 2, 0)))
    out_x = o.transpose(1, 0, 3, 2, 4).reshape(bsz, t_len, C_WIDTH)
    out_c = None
    if need_ctx:
        s = jnp.einsum('bqhd,bkhd->bhqk', qc, kc).astype(F32) * scale
        out_c = jnp.einsum('bhqk,bkhd->bqhd', jax.nn.softmax(s, axis=-1), vc.astype(F32)).reshape(bsz, -1, C_WIDTH)
    return out_x, out_c


def _lti_combine(e1, e2):
    a1, b1 = e1
    a2, b2 = e2
    return a1 * a2, a2 * b1 + b2


def lti_scan(u, lam_bar, b_bar, s0):
    bu = jnp.einsum('gpc,btgc->btgp', b_bar, u.astype(jnp.complex64))
    bu = bu.at[:, 0].add(lam_bar * s0)
    _, xs = lax.associative_scan(_lti_combine, (jnp.broadcast_to(lam_bar, bu.shape), bu), axis=1)
    return xs, xs[:, -1]


def s5_mixer(u_x, u_c, lam_re, lam_im, log_step, b_re, b_im, c_re, c_im, d_skip, glu_w, glu_b, need_ctx):
    lam = lax.complex(lam_re.astype(F32), lam_im.astype(F32))
    lam_bar = jnp.exp(lam * jnp.exp(log_step.astype(F32))[..., None])
    b_cplx = lax.complex(b_re.astype(F32), b_im.astype(F32))
    c_cplx = lax.complex(c_re.astype(F32), c_im.astype(F32))
    b_bar = ((lam_bar - 1.0) / lam)[..., None] * b_cplx[None]

    def groups(v):
        return v.astype(F32).reshape(v.shape[0], v.shape[1], S5_GROUPS, S5_GROUP)

    def fl(v):
        return jnp.flip(v, axis=1)

    ux, uc = groups(u_x), groups(u_c)
    s0 = jnp.zeros((ux.shape[0], S5_GROUPS, S5_STATE), jnp.complex64)
    xs_cf, s_cf = lti_scan(uc, lam_bar[0], b_bar[0], s0)
    xs_xf, _ = lti_scan(ux, lam_bar[0], b_bar[0], s_cf)
    xs_cb, s_cb = lti_scan(fl(uc), lam_bar[1], b_bar[1], s0)
    xs_xb, _ = lti_scan(fl(ux), lam_bar[1], b_bar[1], s_cb)

    def finish(xf, xb, u):
        y = jnp.einsum('gcp,btgp->btgc', c_cplx, xf + fl(xb)).real + d_skip.astype(F32).reshape(S5_GROUPS, S5_GROUP) * u
        y = jax.nn.gelu(y.reshape(u.shape[0], u.shape[1], S5_WIDTH))
        a, gt = jnp.split(y @ glu_w + glu_b, 2, axis=-1)
        return a * jax.nn.sigmoid(gt)

    out_x = finish(xs_xf, xs_xb, ux)
    out_c = finish(xs_cf, xs_cb, uc) if need_ctx else None
    return out_x, out_c


def hybrid_mixer(hx, hc, row, col, lb, w_in, hgrn_norm, conv_w, conv_b, a_log, dt_bias, m_d, m_norm, rpb, lam_re, lam_im, log_step, b_re, b_im, c_re, c_im, s5_d, glu_w, glu_b, proj_a, proj_b, proj_c, w_out, need_ctx):
    px = split_cols(hx @ w_in)
    pc = split_cols(hc @ w_in)
    a_x, a_c = hgrn2_mixer(px[0], px[1], px[2], px[3], px[4], pc[0], pc[1], pc[2], pc[3], pc[4], lb, hgrn_norm, need_ctx)
    b_x, b_c = mamba2_mixer(px[5], px[6], px[7], pc[5], pc[6], pc[7], row, col, conv_w, conv_b, a_log, dt_bias, m_d, m_norm, need_ctx)
    c_x, c_c = na_mixer(px[8], pc[8], rpb, need_ctx)
    d_x, d_c = s5_mixer(px[9], pc[9], lam_re, lam_im, log_step, b_re, b_im, c_re, c_im, s5_d, glu_w, glu_b, need_ctx)

    def merge(ya, yb, yc, yd, gate_logits):
        g = jax.nn.sigmoid(gate_logits.astype(F32)).reshape(*gate_logits.shape[:-1], N_BRANCH, D_MODEL)
        y = g[..., 0, :] * (ya @ proj_a) + g[..., 1, :] * (yb @ proj_b) + g[..., 2, :] * (yc @ proj_c) + g[..., 3, :] * yd
        return (y @ w_out).astype(hx.dtype)

    y_x = merge(a_x, b_x, c_x, d_x, px[10])
    y_c = merge(a_c, b_c, c_c, d_c, pc[10]) if need_ctx else None
    return y_x, y_c


def moe_ffn(h, w_router, b_router, w1, b1, w2, b2):
    n_tok, d = h.shape
    logits = (h @ w_router + b_router).astype(F32)
    top_val, top_idx = lax.top_k(logits, TOP_K)
    gate = jax.nn.softmax(top_val, axis=-1)
    n_assign = n_tok * TOP_K
    e_flat = top_idx.reshape(-1)
    t_flat = jnp.arange(n_assign, dtype=jnp.int32) // TOP_K
    order = jnp.argsort(e_flat)
    e_sorted = e_flat[order]
    counts = jnp.bincount(e_flat, length=N_EXPERTS)
    padded = (counts + MOE_BLOCK - 1) // MOE_BLOCK * MOE_BLOCK
    start = jnp.cumsum(counts) - counts
    pend = jnp.cumsum(padded)
    pstart = pend - padded
    dest = pstart[e_sorted] + jnp.arange(n_assign) - start[e_sorted]
    n_rows = -(-n_assign // MOE_BLOCK) * MOE_BLOCK + N_EXPERTS * MOE_BLOCK
    n_blocks = n_rows // MOE_BLOCK
    buf_tok = jnp.full((n_rows,), n_tok, jnp.int32).at[dest].set(t_flat[order])
    buf_gate = jnp.zeros((n_rows,), F32).at[dest].set(gate.reshape(-1)[order])
    block_exp = jnp.minimum(jnp.searchsorted(pend, jnp.arange(n_blocks) * MOE_BLOCK, side='right'), N_EXPERTS - 1)
    h_pad = jnp.concatenate([h, jnp.zeros((1, d), h.dtype)], axis=0)

    def expert_block(args):
        tok, e = args
        hb = h_pad[tok] @ w1[e] + b1[e]
        glu = jnp.minimum(hb[:, ::2], SWIGLU_LIMIT)
        lin = jnp.clip(hb[:, 1::2], -SWIGLU_LIMIT, SWIGLU_LIMIT)
        act = glu * jax.nn.sigmoid(SWIGLU_ALPHA * glu) * (lin + 1.0)
        return act @ w2[e] + b2[e]

    yb = lax.map(expert_block, (buf_tok.reshape(n_blocks, MOE_BLOCK), block_exp)).reshape(n_rows, d)
    y = jax.ops.segment_sum(yb.astype(F32) * buf_gate[:, None], buf_tok, num_segments=n_tok + 1)[:n_tok]
    return y.astype(h.dtype)


def setup_inputs(seed: int = 0) -> dict:
    key = jax.random.key(seed)
    ks = iter(jax.random.split(key, 40))
    D = D_MODEL

    def nrm(shape, scale):
        return jax.random.normal(next(ks), shape, F32) * scale

    def unif(shape, lo, hi):
        return jax.random.uniform(next(ks), shape, F32, lo, hi)

    x = nrm((BATCH, SEQ, D), 1.0)
    c = nrm((BATCH, D), 1.0)
    ctx = nrm((BATCH, CTX_LEN, D), 1.0)
    c_ctx = nrm((D,), 1.0)
    ada_w = nrm((DEPTH, D, 6 * D), 0.5 * D ** -0.5)
    ada_b = nrm((DEPTH, 6 * D), 0.02)
    norm_mix = 1.0 + nrm((DEPTH, D), 0.02)
    norm_ffn = 1.0 + nrm((DEPTH, D), 0.02)
    w_in = nrm((DEPTH, D, D_IN), D ** -0.5)
    hgrn_lb_logits = nrm((DEPTH, A_WIDTH), 0.5)
    hgrn_norm = 1.0 + nrm((DEPTH, A_DV), 0.02)
    mamba_conv_w = nrm((DEPTH, B_CONV, B_CONV_DIM), B_CONV ** -0.5)
    mamba_conv_b = nrm((DEPTH, B_CONV_DIM), 0.02)
    mamba_a_log = jnp.log(unif((DEPTH, 2, B_HEADS), 1.0, 16.0))
    dt0 = jnp.exp(unif((DEPTH, 2, B_HEADS), math.log(1e-3), math.log(1e-1)))
    mamba_dt_bias = dt0 + jnp.log(-jnp.expm1(-dt0))
    mamba_d = 1.0 + nrm((DEPTH, B_HEADS), 0.1)
    mamba_norm = 1.0 + nrm((DEPTH, B_INNER), 0.02)
    na_rpb = nrm((DEPTH, C_HEADS, 2 * NA_KH - 1, 2 * NA_KW - 1), 0.02)
    s5_lam_re = -0.5 + nrm((DEPTH, 2, S5_GROUPS, S5_STATE), 0.01)
    s5_lam_im = math.pi * jnp.arange(S5_STATE, dtype=F32) + nrm((DEPTH, 2, S5_GROUPS, S5_STATE), 0.01)
    s5_log_step = unif((DEPTH, 2, S5_GROUPS), math.log(1e-3), math.log(1e-1))
    s5_b_re = nrm((DEPTH, S5_GROUPS, S5_STATE, S5_GROUP), (2 * S5_GROUP) ** -0.5)
    s5_b_im = nrm((DEPTH, S5_GROUPS, S5_STATE, S5_GROUP), (2 * S5_GROUP) ** -0.5)
    s5_c_re = nrm((DEPTH, S5_GROUPS, S5_GROUP, S5_STATE), (2 * S5_STATE) ** -0.5)
    s5_c_im = nrm((DEPTH, S5_GROUPS, S5_GROUP, S5_STATE), (2 * S5_STATE) ** -0.5)
    s5_d = nrm((DEPTH, S5_WIDTH), 1.0)
    s5_glu_w = nrm((DEPTH, S5_WIDTH, 2 * D), S5_WIDTH ** -0.5)
    s5_glu_b = nrm((DEPTH, 2 * D), 0.02)
    proj_a = nrm((DEPTH, A_WIDTH, D), A_WIDTH ** -0.5)
    proj_b = nrm((DEPTH, B_INNER, D), B_INNER ** -0.5)
    proj_c = nrm((DEPTH, C_WIDTH, D), C_WIDTH ** -0.5)
    w_out = nrm((DEPTH, D, D), D ** -0.5)
    router_w = nrm((DEPTH, D, N_EXPERTS), D ** -0.5)
    router_b = nrm((DEPTH, N_EXPERTS), 0.01)
    exp_w1 = nrm((DEPTH, N_EXPERTS, D, 2 * D_EXPERT), D ** -0.5)
    exp_b1 = nrm((DEPTH, N_EXPERTS, 2 * D_EXPERT), 0.02)
    exp_w2 = nrm((DEPTH, N_EXPERTS, D_EXPERT, D), D_EXPERT ** -0.5)
    exp_b2 = nrm((DEPTH, N_EXPERTS, D), 0.02)
    final_norm = 1.0 + nrm((D,), 0.02)
    return {'x': x, 'c': c, 'ctx': ctx, 'c_ctx': c_ctx, 'ada_w': ada_w, 'ada_b': ada_b, 'norm_mix': norm_mix, 'norm_ffn': norm_ffn, 'w_in': w_in, 'hgrn_lb_logits': hgrn_lb_logits, 'hgrn_norm': hgrn_norm, 'mamba_conv_w': mamba_conv_w, 'mamba_conv_b': mamba_conv_b, 'mamba_a_log': mamba_a_log, 'mamba_dt_bias': mamba_dt_bias, 'mamba_d': mamba_d, 'mamba_norm': mamba_norm, 'na_rpb': na_rpb, 's5_lam_re': s5_lam_re, 's5_lam_im': s5_lam_im, 's5_log_step': s5_log_step, 's5_b_re': s5_b_re, 's5_b_im': s5_b_im, 's5_c_re': s5_c_re, 's5_c_im': s5_c_im, 's5_d': s5_d, 's5_glu_w': s5_glu_w, 's5_glu_b': s5_glu_b, 'proj_a': proj_a, 'proj_b': proj_b, 'proj_c': proj_c, 'w_out': w_out, 'router_w': router_w, 'router_b': router_b, 'exp_w1': exp_w1, 'exp_b1': exp_b1, 'exp_w2': exp_w2, 'exp_b2': exp_b2, 'final_norm': final_norm}


def reference(x, c, ctx, c_ctx, ada_w, ada_b, norm_mix, norm_ffn, w_in, hgrn_lb_logits, hgrn_norm, mamba_conv_w, mamba_conv_b, mamba_a_log, mamba_dt_bias, mamba_d, mamba_norm, na_rpb, s5_lam_re, s5_lam_im, s5_log_step, s5_b_re, s5_b_im, s5_c_re, s5_c_im, s5_d, s5_glu_w, s5_glu_b, proj_a, proj_b, proj_c, w_out, router_w, router_b, exp_w1, exp_b1, exp_w2, exp_b2, final_norm):
    bsz, t_len, _ = x.shape
    pos = jnp.arange(t_len)
    row, col = pos // GRID_W, pos % GRID_W
    lb_all = jnp.cumsum(jax.nn.softmax(hgrn_lb_logits.astype(F32), axis=0), axis=0)
    lb_all = lb_all - lb_all[0:1]
    cond_x = jax.nn.silu(c)
    cond_c = jax.nn.silu(c_ctx)[None]
    xc = ctx
    for l in range(DEPTH):
        need_ctx = l < DEPTH - 1
        mx = jnp.split(cond_x @ ada_w[l] + ada_b[l], 6, axis=-1)
        mc = jnp.split(cond_c @ ada_w[l] + ada_b[l], 6, axis=-1)
        hx = modulate(rmsnorm(x, norm_mix[l]), mx[0], mx[1])
        hc = modulate(rmsnorm(xc, norm_mix[l]), mc[0], mc[1])
        y_x, y_c = hybrid_mixer(hx, hc, row, col, lb_all[l], w_in[l], hgrn_norm[l], mamba_conv_w[l], mamba_conv_b[l], mamba_a_log[l], mamba_dt_bias[l], mamba_d[l], mamba_norm[l], na_rpb[l], s5_lam_re[l], s5_lam_im[l], s5_log_step[l], s5_b_re[l], s5_b_im[l], s5_c_re[l], s5_c_im[l], s5_d[l], s5_glu_w[l], s5_glu_b[l], proj_a[l], proj_b[l], proj_c[l], w_out[l], need_ctx)
        x = x + mx[2][:, None] * y_x
        hx = modulate(rmsnorm(x, norm_ffn[l]), mx[3], mx[4])
        if need_ctx:
            xc = xc + mc[2][:, None] * y_c
            hc = modulate(rmsnorm(xc, norm_ffn[l]), mc[3], mc[4])
            n_x = bsz * t_len
            f = moe_ffn(jnp.concatenate([hx.reshape(n_x, D_MODEL), hc.reshape(-1, D_MODEL)], axis=0), router_w[l], router_b[l], exp_w1[l], exp_b1[l], exp_w2[l], exp_b2[l])
            x = x + mx[5][:, None] * f[:n_x].reshape(x.shape)
            xc = xc + mc[5][:, None] * f[n_x:].reshape(xc.shape)
        else:
            f = moe_ffn(hx.reshape(-1, D_MODEL), router_w[l], router_b[l], exp_w1[l], exp_b1[l], exp_w2[l], exp_b2[l])
            x = x + mx[5][:, None] * f.reshape(x.shape)
    return rmsnorm(x, final_norm)
```

```python
import functools
import math

import numpy as np
import jax
import jax.numpy as jnp
from jax import lax
from jax.experimental import pallas as pl
from jax.experimental.pallas import tpu as pltpu

F32 = jnp.float32
BF16 = jnp.bfloat16

D_MODEL = 1024
GRID_W = 64
EPS = 1e-6
N_BRANCH = 4
A_HEADS = 4
A_DK = 128
A_WIDTH = 512
B_HEADS = 8
B_HEADDIM = 64
B_INNER = 512
B_GROUPS = 2
B_STATE = 128
B_CONV = 5
B_CONV_DIM = 1024
ROPE_THETA = 10000.0
C_HEADS = 8
C_HEADDIM = 64
C_WIDTH = 512
NA_KH = 8
NA_KW = 16
S5_GROUP = 16
S5_GROUPS = 32
S5_WIDTH = 512
S5_STATE = 64
N_EXPERTS = 32
TOP_K = 4
D_EXPERT = 1024
SWIGLU_LIMIT = 7.0
SWIGLU_ALPHA = 1.702

LANE = 128
GLA_BLK = 16
CHUNK = 128
S5_CHUNK = 16
NA_WIN = 10
MOE_TM = 512
NEG_BIG = -1e30
VMEM_LIMIT = 56 * 1024 * 1024

CB_Q, CB_FF, CB_FB, CB_I, CB_G = 0, 4, 8, 12, 16
CB_MZ, CB_XBC, CB_QKV, CB_S5, CB_GATE, CB_DT = 20, 24, 32, 44, 48, 80
Z_WIDTH = 81 * LANE
Z_TN = 9 * LANE


def _dot(a, b):
    return jnp.dot(a, b, preferred_element_type=F32)


def _dot_nt(a, b):
    return lax.dot_general(a, b, (((1,), (1,)), ((), ())), preferred_element_type=F32)


def _hi_lo(x):
    hi = x.astype(BF16)
    lo = (x - hi.astype(F32)).astype(BF16)
    return hi, lo


def _sel_dot(mat, x):
    hi, lo = _hi_lo(x)
    return _dot(mat, hi) + _dot(mat, lo)


def _dot_sel(x, mat):
    hi, lo = _hi_lo(x)
    return _dot(hi, mat) + _dot(lo, mat)


def _cparams(sem):
    return pltpu.CompilerParams(dimension_semantics=sem, vmem_limit_bytes=VMEM_LIMIT)


def _prep_kernel(*refs, has_moe, final):
    if has_moe:
        x_ref, y_ref, modp_ref, mod_ref, g_ref = refs[:5]
        outs = refs[5:]
    else:
        x_ref, mod_ref, g_ref = refs[:3]
        outs = refs[3:]
    x = x_ref[0]
    if has_moe:
        f = (y_ref[0, 0].astype(F32) + y_ref[1, 0].astype(F32)) + (y_ref[2, 0].astype(F32) + y_ref[3, 0].astype(F32))
        x = x + modp_ref[0, 0, 5:6, :] * f
    y = x * lax.rsqrt(jnp.mean(x * x, axis=-1, keepdims=True) + EPS) * g_ref[...]
    if final:
        outs[0][0] = y
        return
    h = y * (1.0 + mod_ref[0, 0, 1:2, :]) + mod_ref[0, 0, 0:1, :]
    if has_moe:
        outs[0][0] = x
        outs[1][0] = h.astype(BF16)
    else:
        outs[0][0] = h.astype(BF16)


def _prep(x, mod, g, n_ctx, tr, ymoe=None, mod_prev=None, final=False):
    bsz, s, d = x.shape
    nct = n_ctx // tr
    has_moe = ymoe is not None
    kind = lambda j: jnp.minimum(j // nct, 1)
    if final:
        grid = (bsz, (s - n_ctx) // tr)
        off = nct
    else:
        grid = (bsz, s // tr)
        off = 0
    x_spec = pl.BlockSpec((1, tr, d), lambda b, j: (b, j + off, 0))
    mod_spec = pl.BlockSpec((1, 1, 6, d), lambda b, j: (b, kind(j + off), 0, 0))
    g_spec = pl.BlockSpec((1, d), lambda b, j: (0, 0))
    in_specs, args = [x_spec], [x]
    if has_moe:
        in_specs += [pl.BlockSpec((4, 1, tr, d), lambda b, j: (0, b, j + off, 0)), mod_spec]
        args += [ymoe, mod_prev]
    in_specs += [mod_spec, g_spec]
    args += [mod, g.reshape(1, d)]
    if final:
        out_shape = [jax.ShapeDtypeStruct((bsz, s - n_ctx, d), F32)]
        out_specs = [pl.BlockSpec((1, tr, d), lambda b, j: (b, j, 0))]
    elif has_moe:
        out_shape = [jax.ShapeDtypeStruct((bsz, s, d), F32), jax.ShapeDtypeStruct((bsz, s, d), BF16)]
        out_specs = [x_spec, x_spec]
    else:
        out_shape = [jax.ShapeDtypeStruct((bsz, s, d), BF16)]
        out_specs = [x_spec]
    return pl.pallas_call(
        functools.partial(_prep_kernel, has_moe=has_moe, final=final),
        grid=grid, in_specs=in_specs, out_specs=out_specs, out_shape=out_shape,
        compiler_params=_cparams(("parallel", "parallel")),
    )(*args)


def _matmul_kernel(h_ref, w_ref, z_ref):
    z_ref[...] = _dot(h_ref[...], w_ref[...]).astype(z_ref.dtype)


def _in_proj(h2d, w):
    m, k = h2d.shape
    n = w.shape[1]
    tm = next(t for t in (1024, 512, 256, 128) if m % t == 0)
    return pl.pallas_call(
        _matmul_kernel,
        grid=(n // Z_TN, m // tm),
        in_specs=[pl.BlockSpec((tm, k), lambda j, i: (i, 0)), pl.BlockSpec((k, Z_TN), lambda j, i: (0, j))],
        out_specs=pl.BlockSpec((tm, Z_TN), lambda j, i: (i, j)),
        out_shape=jax.ShapeDtypeStruct((m, n), BF16),
        compiler_params=_cparams(("parallel", "parallel")),
    )(h2d, w)


def _gla_kernel(q_ref, ff_ref, fb_ref, i_ref, g_ref, p_ref, o_ref,
                b_scr, k_scr, v_scr, qt_scr, dec_scr, od_scr, x_scr, *, n_ctx, s_len):
    nchunk = s_len // CHUNK
    nblk = s_len // GLA_BLK
    nb_ctx = n_ctx // GLA_BLK
    bpc = CHUNK // GLA_BLK
    scale = A_DK ** -0.5
    log_lb = p_ref[0, 1:2, :]
    log_1mlb = p_ref[0, 2:3, :]
    omlb = p_ref[0, 3:4, :]
    ng = p_ref[0, 4:5, :]
    ri = lax.broadcasted_iota(jnp.int32, (CHUNK, CHUNK), 0)
    ci = lax.broadcasted_iota(jnp.int32, (CHUNK, CHUNK), 1)
    same = (ri // GLA_BLK) == (ci // GLA_BLK)
    tri = (jnp.where(same, jnp.where(ci <= ri, 1.0, 0.0), 0.0).astype(BF16),
           jnp.where(same, jnp.where(ci >= ri, 1.0, 0.0), 0.0).astype(BF16))
    blk_sum = jnp.where(same, 1.0, 0.0).astype(BF16)
    cblk = ci // GLA_BLK

    def bulk(c, _):
        r0 = pl.multiple_of(c * CHUNK, CHUNK)
        q = q_ref[0, pl.ds(r0, CHUNK), :].astype(F32) * scale
        v = i_ref[0, pl.ds(r0, CHUNK), :].astype(F32)
        v_scr[pl.ds(r0, CHUNK), :] = v
        v_t = v.T
        for d, fz_ref in enumerate((ff_ref, fb_ref)):
            fz = fz_ref[0, pl.ds(r0, CHUNK), :].astype(F32)
            e = jnp.exp(-jnp.abs(fz))
            log_sig = jnp.minimum(fz, 0.0) - jnp.log1p(e)
            k = omlb * (jnp.where(fz >= 0, e, 1.0) / (1.0 + e))
            t2 = log_1mlb + log_sig
            lf = jnp.maximum(log_lb, t2) + jnp.log1p(jnp.exp(-jnp.abs(log_lb - t2)))
            hi, lo = _hi_lo(lf)
            bcum = _dot(tri[d], hi) + _dot(tri[d], lo)
            tot = _dot(blk_sum, hi) + _dot(blk_sum, lo)
            b_scr[d, pl.ds(r0, CHUNK), :] = bcum
            k_scr[d, pl.ds(r0, CHUNK), :] = k
            qt_scr[d, pl.ds(r0, CHUNK), :] = (q * jnp.exp(bcum)).astype(BF16)
            dec_scr[d, pl.ds(r0, CHUNK), :] = jnp.exp(tot)
            kt = (k * jnp.exp(tot - bcum)).astype(BF16)
            for j in range(bpc):
                x_scr[d, c * bpc + j] = _dot(jnp.where(cblk == j, v_t, 0.0).astype(BF16), kt).astype(BF16)
        return 0

    lax.fori_loop(0, nchunk, bulk, 0)

    tio = lax.broadcasted_iota(jnp.int32, (GLA_BLK, LANE), 0)
    ones_red = jnp.ones((LANE, LANE), BF16)

    def diag(i, _):
        r0 = pl.multiple_of(i * GLA_BLK, GLA_BLK)
        qq = q_ref[0, pl.ds(r0, GLA_BLK), :].astype(F32) * scale
        for d in range(2):
            bb = b_scr[d, pl.ds(r0, GLA_BLK), :]
            pieces = []
            for s in range(GLA_BLK):
                bs = b_scr[d, pl.ds(r0 + s, 1), :]
                ks = k_scr[d, pl.ds(r0 + s, 1), :]
                msk = (tio >= s) if d == 0 else (tio <= s)
                w = jnp.exp(jnp.where(msk, bb - bs, -jnp.inf))
                pieces.append(((qq * w) * ks).astype(BF16))
            red = _dot(jnp.concatenate(pieces, axis=0), ones_red)
            od = jnp.zeros((GLA_BLK, LANE), F32)
            for s in range(GLA_BLK):
                od = od + red[s * GLA_BLK:(s + 1) * GLA_BLK, :] * v_scr[pl.ds(r0 + s, 1), :]
            od_scr[d, pl.ds(r0, GLA_BLK), :] = od
        return 0

    lax.fori_loop(0, nblk, diag, 0)

    def scan(i, carry):
        new = []
        order = (i, jnp.where(i < nb_ctx, nb_ctx - 1 - i, nblk - 1 - (i - nb_ctx)))
        for d in range(2):
            j = order[d]
            st = carry[d]
            ds_blk = x_scr[d, j].astype(F32)
            x_scr[d, j] = st.astype(BF16)
            new.append(st * dec_scr[d, pl.ds(j * GLA_BLK, 1), :] + ds_blk)
        return tuple(new)

    zero_state = jnp.zeros((LANE, LANE), F32)
    lax.fori_loop(0, nblk, scan, (zero_state, zero_state))

    def emit(i, _):
        r0 = pl.multiple_of(i * GLA_BLK, GLA_BLK)
        o = od_scr[0, pl.ds(r0, GLA_BLK), :] + od_scr[1, pl.ds(r0, GLA_BLK), :]
        o = o + _dot_nt(qt_scr[0, pl.ds(r0, GLA_BLK), :], x_scr[0, i])
        o = o + _dot_nt(qt_scr[1, pl.ds(r0, GLA_BLK), :], x_scr[1, i])
        o = o * lax.rsqrt(jnp.mean(o * o, axis=-1, keepdims=True) + EPS) * ng
        g = g_ref[0, pl.ds(r0, GLA_BLK), :].astype(F32)
        o_ref[0, pl.ds(r0, GLA_BLK), :] = (o * (g * jax.nn.sigmoid(g))).astype(BF16)
        return 0

    lax.fori_loop(0, nblk, emit, 0)


def _gla(z, params, n_ctx):
    bsz, s, _ = z.shape
    spec = lambda cb: pl.BlockSpec((1, s, LANE), lambda b, h: (b, 0, cb + h))
    return pl.pallas_call(
        functools.partial(_gla_kernel, n_ctx=n_ctx, s_len=s),
        grid=(bsz, A_HEADS),
        in_specs=[spec(CB_Q), spec(CB_FF), spec(CB_FB), spec(CB_I), spec(CB_G),
                  pl.BlockSpec((1, 8, LANE), lambda b, h: (h, 0, 0))],
        out_specs=pl.BlockSpec((1, s, LANE), lambda b, h: (b, 0, h)),
        out_shape=jax.ShapeDtypeStruct((bsz, s, A_WIDTH), BF16),
        scratch_shapes=[
            pltpu.VMEM((2, s, LANE), F32), pltpu.VMEM((2, s, LANE), F32), pltpu.VMEM((s, LANE), F32),
            pltpu.VMEM((2, s, LANE), BF16), pltpu.VMEM((2, s, LANE), F32), pltpu.VMEM((2, s, LANE), F32),
            pltpu.VMEM((2, s // GLA_BLK, LANE, LANE), BF16),
        ],
        compiler_params=_cparams(("parallel", "parallel")),
    )(z, z, z, z, z, params)


def _ssd_kernel(mz_ref, xbc_ref, dt_ref, cw_ref, cb_ref, cos_ref, sin_ref, vec_ref, exp_ref, sel_ref,
                dsk_ref, ng_ref, o_ref, xs_scr, bm_scr, cm_scr, dt_scr, y_scr, *, n_ctx, s_len):
    nchunk = s_len // CHUNK
    nc_ctx = n_ctx // CHUNK
    halo = 16
    ri = lax.broadcasted_iota(jnp.int32, (CHUNK, CHUNK), 0)
    ci = lax.broadcasted_iota(jnp.int32, (CHUNK, CHUNK), 1)
    lane = lax.broadcasted_iota(jnp.int32, (1, LANE), 1)
    dt_bias = vec_ref[0:1, :]
    a_lane = vec_ref[1:2, :]
    dt_valid = lane < 2 * B_HEADS

    wi_r = lax.broadcasted_iota(jnp.int32, (CHUNK, CHUNK + 2 * halo), 0)
    wi_c = lax.broadcasted_iota(jnp.int32, (CHUNK, CHUNK + 2 * halo), 1)
    shift = [jnp.where(wi_c == wi_r + halo + o, 1.0, 0.0).astype(BF16) for o in range(-(B_CONV // 2), B_CONV // 2 + 1)]

    def prep(c, _):
        r0 = pl.multiple_of(c * CHUNK, CHUNK)
        same_prev = jnp.logical_and(c > 0, c != nc_ctx)
        same_next = jnp.logical_and(c < nchunk - 1, c != nc_ctx - 1)
        rp = pl.multiple_of(jnp.maximum(r0 - halo, 0), halo)
        rn = pl.multiple_of(jnp.minimum(r0 + CHUNK, s_len - halo), halo)
        prev = jnp.where(same_prev, xbc_ref[0, pl.ds(rp, halo), :].astype(F32), 0.0).astype(BF16)
        nxt = jnp.where(same_next, xbc_ref[0, pl.ds(rn, halo), :].astype(F32), 0.0).astype(BF16)
        win = jnp.concatenate([prev, xbc_ref[0, pl.ds(r0, CHUNK), :], nxt], axis=0)
        u = jnp.zeros((CHUNK, B_CONV_DIM), F32) + cb_ref[...]
        for j in range(B_CONV):
            u = u + _dot(shift[j], win) * cw_ref[j:j + 1, :]
        u = u * jax.nn.sigmoid(u)
        xs_scr[pl.ds(r0, CHUNK), :] = u[:, :B_INNER].astype(BF16)
        cos = cos_ref[pl.ds(r0, CHUNK), :]
        sin = sin_ref[pl.ds(r0, CHUNK), :]
        first = (lane % 64) < 32
        for k, scr in ((0, bm_scr), (1, cm_scr)):
            for g in range(B_GROUPS):
                lo = B_INNER + (k * B_GROUPS + g) * B_STATE
                vv = u[:, lo:lo + B_STATE]
                partner = jnp.where(first, pltpu.roll(vv, LANE - 32, axis=1), pltpu.roll(vv, 32, axis=1))
                scr[pl.ds(r0, CHUNK), g * B_STATE:(g + 1) * B_STATE] = (vv * cos + partner * sin).astype(BF16)
        xd = dt_ref[0, pl.ds(r0, CHUNK), :].astype(F32) + dt_bias
        dt = jnp.maximum(xd, 0.0) + jnp.log1p(jnp.exp(-jnp.abs(xd)))
        dt_scr[pl.ds(r0, CHUNK), :] = jnp.where(dt_valid, dt, 0.0)
        y_scr[pl.ds(r0, CHUNK), :] = u[:, :B_INNER] * dsk_ref[...]
        return 0

    lax.fori_loop(0, nchunk, prep, 0)

    tri = (jnp.where(ci <= ri, 1.0, 0.0).astype(BF16), jnp.where(ci >= ri, 1.0, 0.0).astype(BF16))
    keep = (ci <= ri, ci >= ri)
    lane_lo = lax.broadcasted_iota(jnp.int32, (1, LANE), 1) < B_HEADDIM
    gw = B_INNER // B_GROUPS

    def step(i, carry):
        new = []
        order = (i, jnp.where(i < nc_ctx, nc_ctx - 1 - i, nchunk - 1 - (i - nc_ctx)))
        for d in range(2):
            c = order[d]
            r0 = pl.multiple_of(c * CHUNK, CHUNK)
            st = carry[d]
            dt = dt_scr[pl.ds(r0, CHUNK), :]
            acum = _sel_dot(tri[d], dt * a_lane)
            acum_t = acum.T
            expand = exp_ref[d]
            xdt = xs_scr[pl.ds(r0, CHUNK), :].astype(F32) * _dot_sel(dt, expand)
            ea_x = _dot_sel(jnp.exp(acum), expand)
            last = CHUNK - 1 if d == 0 else 0
            total = acum[last:last + 1, :]
            w_x = _dot_sel(jnp.exp(total - acum), expand)
            dec_row = ea_x[last:last + 1, :]
            xw_b = (xdt * w_x).astype(BF16)
            hi, lo = _hi_lo(acum)
            y_parts = []
            s_parts = []
            for g in range(B_GROUPS):
                bm = bm_scr[pl.ds(r0, CHUNK), g * B_STATE:(g + 1) * B_STATE]
                cm = cm_scr[pl.ds(r0, CHUNK), g * B_STATE:(g + 1) * B_STATE]
                gmat = _dot_nt(cm, bm)
                y_off = _dot(cm, st[:, g * gw:(g + 1) * gw].astype(BF16))
                pair_out = []
                for pp in range(2):
                    p = g * 2 + pp
                    xp = xdt[:, p * LANE:(p + 1) * LANE]
                    acc = jnp.zeros((CHUNK, LANE), F32)
                    for hh in range(2):
                        j = d * B_HEADS + 2 * p + hh
                        col = _dot(hi, sel_ref[j]) + _dot(lo, sel_ref[j])
                        seg = col - acum_t[j:j + 1, :]
                        m = (gmat * jnp.exp(jnp.where(keep[d], seg, -jnp.inf))).astype(BF16)
                        xm = jnp.where(lane_lo, xp, 0.0) if hh == 0 else jnp.where(lane_lo, 0.0, xp)
                        acc = acc + _dot(m, xm.astype(BF16))
                    pair_out.append(acc)
                y_parts.append(jnp.concatenate(pair_out, axis=1) + y_off * ea_x[:, g * gw:(g + 1) * gw])
                bm_t = bm.astype(F32).T.astype(BF16)
                s_parts.append(_dot(bm_t, xw_b[:, g * gw:(g + 1) * gw]))
            y_scr[pl.ds(r0, CHUNK), :] += jnp.concatenate(y_parts, axis=1)
            new.append(st * dec_row + jnp.concatenate(s_parts, axis=1))
        return tuple(new)

    zero_state = jnp.zeros((B_STATE, B_INNER), F32)
    lax.fori_loop(0, nchunk, step, (zero_state, zero_state))

    def finish(c, _):
        r0 = pl.multiple_of(c * CHUNK, CHUNK)
        zz = mz_ref[0, pl.ds(r0, CHUNK), :].astype(F32)
        y = y_scr[pl.ds(r0, CHUNK), :] * (zz * jax.nn.sigmoid(zz))
        y = y * lax.rsqrt(jnp.mean(y * y, axis=-1, keepdims=True) + EPS) * ng_ref[...]
        o_ref[0, pl.ds(r0, CHUNK), :] = y.astype(BF16)
        return 0

    lax.fori_loop(0, nchunk, finish, 0)


def _ssd(z, p, n_ctx):
    bsz, s, _ = z.shape
    const = lambda shape: pl.BlockSpec(shape, lambda b: (0,) * len(shape))
    return pl.pallas_call(
        functools.partial(_ssd_kernel, n_ctx=n_ctx, s_len=s),
        grid=(bsz,),
        in_specs=[
            pl.BlockSpec((1, s, B_INNER), lambda b: (b, 0, CB_MZ * LANE // B_INNER)),
            pl.BlockSpec((1, s, B_CONV_DIM), lambda b: (b, 0, CB_XBC * LANE // B_CONV_DIM)),
            pl.BlockSpec((1, s, LANE), lambda b: (b, 0, CB_DT)),
            const((8, B_CONV_DIM)), const((1, B_CONV_DIM)), const((s, LANE)), const((s, LANE)),
            const((8, LANE)), const((2, LANE, B_INNER)), const((2 * B_HEADS, LANE, LANE)),
            const((1, B_INNER)), const((1, B_INNER)),
        ],
        out_specs=pl.BlockSpec((1, s, B_INNER), lambda b: (b, 0, 0)),
        out_shape=jax.ShapeDtypeStruct((bsz, s, B_INNER), BF16),
        scratch_shapes=[
            pltpu.VMEM((s, B_INNER), BF16), pltpu.VMEM((s, B_GROUPS * B_STATE), BF16),
            pltpu.VMEM((s, B_GROUPS * B_STATE), BF16), pltpu.VMEM((s, LANE), F32), pltpu.VMEM((s, B_INNER), F32),
        ],
        compiler_params=_cparams(("parallel",)),
    )(z, z, z, p["conv_w"], p["conv_b"], p["cos"], p["sin"], p["vec"], p["expand"], p["sel"], p["dskip"], p["norm"])


def _na_kernel(q_ref, k_ref, v_ref, bias_ref, o_ref, vt_scr, *, n_ctx, s_len):
    rows = (s_len - n_ctx) // GRID_W
    n_pairs = rows // 2
    n_vt = s_len // LANE
    nct = n_ctx // LANE
    win = NA_WIN * GRID_W
    scale = C_HEADDIM ** -0.5
    lane = lax.broadcasted_iota(jnp.int32, (1, LANE), 1)
    head_lanes = (lane < C_HEADDIM, lane >= C_HEADDIM)
    sub_lo = lax.broadcasted_iota(jnp.int32, (LANE, 1), 0) < C_HEADDIM

    def transpose_v(t, _):
        r0 = pl.multiple_of(t * LANE, LANE)
        vt_scr[t] = v_ref[0, pl.ds(r0, LANE), :].astype(F32).T.astype(BF16)
        return 0

    lax.fori_loop(0, n_vt, transpose_v, 0)

    k_ctx = k_ref[0, 0:n_ctx, :]

    qc = q_ref[0, 0:n_ctx, :].astype(F32) * scale
    v_ctx = v_ref[0, 0:n_ctx, :]
    out_c = jnp.zeros((n_ctx, LANE), F32)
    for hh in range(2):
        qm = jnp.where(head_lanes[hh], qc, 0.0).astype(BF16)
        sc = _dot_nt(qm, k_ctx)
        pc = jnp.exp(sc - jnp.max(sc, axis=-1, keepdims=True))
        oc = _dot(pc.astype(BF16), v_ctx) / jnp.sum(pc, axis=-1, keepdims=True)
        out_c = jnp.where(head_lanes[hh], oc, out_c)
    o_ref[0, 0:n_ctx, :] = out_c.astype(BF16)

    def pair(rp, _):
        r0 = 2 * rp
        su = jnp.clip(r0 - NA_KH // 2, 0, rows - NA_WIN)
        q0 = pl.multiple_of(n_ctx + r0 * GRID_W, LANE)
        k0 = pl.multiple_of(n_ctx + su * GRID_W, LANE)
        qp = q_ref[0, pl.ds(q0, LANE), :].astype(F32) * scale
        kw = k_ref[0, pl.ds(k0, win), :]
        vt0 = nct + su // 2
        starts = [jnp.clip(r0 + i - NA_KH // 2, 0, rows - NA_KH) for i in range(2)]
        out_t = []
        for hh in range(2):
            qm = jnp.where(head_lanes[hh], qp, 0.0).astype(BF16)
            st = _dot_nt(kw, qm)
            sc = _dot_nt(k_ctx, qm)
            tiles = []
            for w in range(NA_WIN):
                kr = su + w
                ok = [jnp.logical_and(kr >= starts[i], kr < starts[i] + NA_KH).astype(jnp.int32) for i in range(2)]
                okv = jnp.where(lane < GRID_W, ok[0], ok[1]) > 0
                bias = bias_ref[hh, kr - r0 + NA_KH]
                tiles.append(jnp.where(okv, st[w * GRID_W:(w + 1) * GRID_W, :] + bias, NEG_BIG))
            sw = jnp.concatenate(tiles, axis=0)
            m = jnp.maximum(jnp.max(sw, axis=0, keepdims=True), jnp.max(sc, axis=0, keepdims=True))
            pw = jnp.exp(sw - m)
            pc = jnp.exp(sc - m)
            den = jnp.sum(pw, axis=0, keepdims=True) + jnp.sum(pc, axis=0, keepdims=True)
            pw = pw.astype(BF16)
            pc = pc.astype(BF16)
            acc = jnp.zeros((LANE, LANE), F32)
            for t in range(win // LANE):
                acc = acc + _dot(vt_scr[vt0 + t], pw[t * LANE:(t + 1) * LANE, :])
            for t in range(nct):
                acc = acc + _dot(vt_scr[t], pc[t * LANE:(t + 1) * LANE, :])
            out_t.append(acc / den)
        o_ref[0, pl.ds(q0, LANE), :] = jnp.where(sub_lo, out_t[0], out_t[1]).T.astype(BF16)
        return 0

    lax.fori_loop(0, n_pairs, pair, 0)


def _na(z, bias, n_ctx):
    bsz, s, _ = z.shape
    spec = lambda cb: pl.BlockSpec((1, s, LANE), lambda b, p: (b, 0, cb + p))
    return pl.pallas_call(
        functools.partial(_na_kernel, n_ctx=n_ctx, s_len=s),
        grid=(bsz, C_HEADS // 2),
        in_specs=[spec(CB_QKV), spec(CB_QKV + 4), spec(CB_QKV + 8),
                  pl.BlockSpec((2, 2 * NA_KH + 2, GRID_W, LANE), lambda b, p: (p, 0, 0, 0))],
        out_specs=pl.BlockSpec((1, s, LANE), lambda b, p: (b, 0, p)),
        out_shape=jax.ShapeDtypeStruct((bsz, s, C_WIDTH), BF16),
        scratch_shapes=[pltpu.VMEM((s // LANE, LANE, LANE), BF16)],
        compiler_params=_cparams(("parallel", "parallel")),
    )(z, z, z, bias)


def _s5_kernel(u_ref, t_ref, q_ref, p_ref, lam_ref, dsk_ref, o_ref, w_scr, z_scr, *, n_ctx, s_len, bsz):
    nj = s_len // S5_CHUNK
    nj_ctx = n_ctx // S5_CHUNK
    n_rows = nj * bsz
    rc = math.gcd(n_rows, 256)
    qmat = q_ref[0]

    def project(r, _):
        r0 = pl.multiple_of(r * rc, rc)
        w_scr[pl.ds(r0, rc), :] = _dot(u_ref[0, pl.ds(r0, rc), :], qmat)
        return 0

    lax.fori_loop(0, n_rows // rc, project, 0)

    lam = [lam_ref[0, k:k + 1, :] for k in range(4)]

    def scan(i, carry):
        order = (i, jnp.where(i < nj_ctx, nj_ctx - 1 - i, nj - 1 - (i - nj_ctx)))
        new = []
        for d in range(2):
            r0 = pl.multiple_of(order[d] * bsz, bsz)
            z_re, z_im = carry[2 * d], carry[2 * d + 1]
            a_re, a_im = lam[2 * d], lam[2 * d + 1]
            c0 = 2 * d * LANE
            z_scr[pl.ds(r0, bsz), c0:c0 + LANE] = z_re
            z_scr[pl.ds(r0, bsz), c0 + LANE:c0 + 2 * LANE] = z_im
            w_re = w_scr[pl.ds(r0, bsz), c0:c0 + LANE]
            w_im = w_scr[pl.ds(r0, bsz), c0 + LANE:c0 + 2 * LANE]
            new.append(a_re * z_re - a_im * z_im + w_re)
            new.append(a_re * z_im + a_im * z_re + w_im)
        return tuple(new)

    z0 = jnp.zeros((bsz, LANE), F32)
    lax.fori_loop(0, nj, scan, (z0, z0, z0, z0))

    tmat = t_ref[0]
    pmat = p_ref[0]
    dsk = dsk_ref[0]

    def emit(r, _):
        r0 = pl.multiple_of(r * rc, rc)
        u = u_ref[0, pl.ds(r0, rc), :]
        y = _dot(u, tmat) + _dot(z_scr[pl.ds(r0, rc), :].astype(BF16), pmat) + dsk * u.astype(F32)
        y = 0.5 * y * (1.0 + jnp.tanh(math.sqrt(2.0 / math.pi) * (y + 0.044715 * (y * y * y))))
        o_ref[0, pl.ds(r0, rc), :] = y.astype(BF16)
        return 0

    lax.fori_loop(0, n_rows // rc, emit, 0)


def _s5(u_pairs, p, n_ctx, s_len, bsz):
    npair, n_rows, width = u_pairs.shape
    blk = lambda shape: pl.BlockSpec((1,) + shape, lambda g: (g,) + (0,) * len(shape))
    return pl.pallas_call(
        functools.partial(_s5_kernel, n_ctx=n_ctx, s_len=s_len, bsz=bsz),
        grid=(npair,),
        in_specs=[blk((n_rows, width)), blk((width, width)), blk((width, width)), blk((width, width)),
                  blk((8, LANE)), blk((1, width))],
        out_specs=blk((n_rows, width)),
        out_shape=jax.ShapeDtypeStruct((npair, n_rows, width), BF16),
        scratch_shapes=[pltpu.VMEM((n_rows, width), F32), pltpu.VMEM((n_rows, width), F32)],
        compiler_params=_cparams(("parallel",)),
    )(u_pairs, p["t"], p["q"], p["p"], p["lam"], p["dsk"])


def _merge_kernel(a_ref, b_ref, c_ref, d_ref, g0_ref, g1_ref, g2_ref, g3_ref, x_ref, mod_ref,
                  pa_ref, pb_ref, pc_ref, gw_ref, gb_ref, wo_ref, nf_ref, rw_ref, rb_ref,
                  xo_ref, h_ref, lg_ref):
    sig = lambda r: jax.nn.sigmoid(r[0].astype(F32))
    glu = _dot(d_ref[0], gw_ref[...]) + gb_ref[...]
    yd = glu[:, :D_MODEL] * jax.nn.sigmoid(glu[:, D_MODEL:])
    y = sig(g0_ref) * _dot(a_ref[0], pa_ref[...]) + sig(g1_ref) * _dot(b_ref[0], pb_ref[...])
    y = y + sig(g2_ref) * _dot(c_ref[0], pc_ref[...]) + sig(g3_ref) * yd
    y = _dot(y.astype(BF16), wo_ref[...])
    x = x_ref[0] + mod_ref[0, 0, 2:3, :] * y
    xo_ref[0] = x
    h = x * lax.rsqrt(jnp.mean(x * x, axis=-1, keepdims=True) + EPS) * nf_ref[...]
    h = h * (1.0 + mod_ref[0, 0, 4:5, :]) + mod_ref[0, 0, 3:4, :]
    h_ref[0] = h.astype(BF16)
    hi, lo = _hi_lo(h)
    lg_ref[0] = _dot(hi, rw_ref[0]) + _dot(lo, rw_ref[0]) + _dot(hi, rw_ref[1]) + rb_ref[...]


def _merge(a, b, c, d, z, x, mod, w, n_ctx, tr):
    bsz, s, dm = x.shape
    nct = n_ctx // tr
    tile = lambda width: pl.BlockSpec((1, tr, width), lambda bb, j: (bb, j, 0))
    gate = lambda k: pl.BlockSpec((1, tr, dm), lambda bb, j: (bb, j, CB_GATE * LANE // dm + k))
    const = lambda shape: pl.BlockSpec(shape, lambda bb, j: (0,) * len(shape))
    return pl.pallas_call(
        _merge_kernel,
        grid=(bsz, s // tr),
        in_specs=[tile(A_WIDTH), tile(B_INNER), tile(C_WIDTH), tile(S5_WIDTH),
                  gate(0), gate(1), gate(2), gate(3), tile(dm),
                  pl.BlockSpec((1, 1, 6, dm), lambda bb, j: (bb, jnp.minimum(j // nct, 1), 0, 0)),
                  const((A_WIDTH, dm)), const((B_INNER, dm)), const((C_WIDTH, dm)),
                  const((S5_WIDTH, 2 * dm)), const((1, 2 * dm)), const((dm, dm)), const((1, dm)),
                  const((2, dm, LANE)), const((1, LANE))],
        out_specs=[tile(dm), tile(dm), tile(LANE)],
        out_shape=[jax.ShapeDtypeStruct((bsz, s, dm), F32), jax.ShapeDtypeStruct((bsz, s, dm), BF16),
                   jax.ShapeDtypeStruct((bsz, s, LANE), F32)],
        compiler_params=_cparams(("parallel", "parallel")),
    )(a, b, c, d, z, z, z, z, x, mod, w["proj_a"], w["proj_b"], w["proj_c"], w["glu_w"], w["glu_b"],
      w["w_out"], w["norm_ffn"], w["router_w"], w["router_b"])


def _moe_kernel(be_ref, na_ref, hs_ref, gate_ref, w1g_ref, w1l_ref, b1g_ref, b1l_ref, w2_ref, b2_ref, o_ref):
    i = pl.program_id(0)

    @pl.when(i < na_ref[0])
    def _():
        hs = hs_ref[...]
        glu = jnp.minimum(_dot(hs, w1g_ref[0]) + b1g_ref[0], SWIGLU_LIMIT)
        lin = jnp.clip(_dot(hs, w1l_ref[0]) + b1l_ref[0], -SWIGLU_LIMIT, SWIGLU_LIMIT)
        act = glu * jax.nn.sigmoid(SWIGLU_ALPHA * glu) * (lin + 1.0)
        y = _dot(act.astype(BF16), w2_ref[0]) + b2_ref[0]
        o_ref[...] = (y * gate_ref[...]).astype(o_ref.dtype)

    @pl.when(i >= na_ref[0])
    def _():
        o_ref[...] = jnp.zeros_like(o_ref)


def _moe_ffn(hs, gate_rows, block_exp, n_active, w):
    n_rows, dm = hs.shape
    tm = MOE_TM
    n_blocks = n_rows // tm
    de = w["w1g"].shape[-1]
    wspec = lambda shape: pl.BlockSpec((1,) + shape, lambda i, be, na: (be[i], 0, 0))
    grid_spec = pltpu.PrefetchScalarGridSpec(
        num_scalar_prefetch=2,
        grid=(n_blocks,),
        in_specs=[pl.BlockSpec((tm, dm), lambda i, be, na: (i, 0)),
                  pl.BlockSpec((tm, 1), lambda i, be, na: (i, 0)),
                  wspec((dm, de)), wspec((dm, de)), wspec((1, de)), wspec((1, de)),
                  wspec((de, dm)), wspec((1, dm))],
        out_specs=pl.BlockSpec((tm, dm), lambda i, be, na: (i, 0)),
    )
    return pl.pallas_call(
        _moe_kernel, grid_spec=grid_spec,
        out_shape=jax.ShapeDtypeStruct((n_rows, dm), BF16),
        compiler_params=_cparams(("arbitrary",)),
    )(block_exp, n_active, hs, gate_rows, w["w1g"], w["w1l"], w["b1g"], w["b1l"], w["w2"], w["b2"])


def _route(logits, n_tok):
    tm = MOE_TM
    top_val, top_idx = lax.top_k(logits, TOP_K)
    gate = jax.nn.softmax(top_val, axis=-1).reshape(-1)
    n_assign = n_tok * TOP_K
    e_flat = top_idx.reshape(-1).astype(jnp.int32)
    ar = jnp.arange(n_assign, dtype=jnp.int32)
    e_sorted, order = lax.sort((e_flat, ar), num_keys=1, is_stable=True)
    _, inv = lax.sort((order, ar), num_keys=1)
    counts = jnp.sum(jax.nn.one_hot(e_flat, N_EXPERTS, dtype=jnp.int32), axis=0)
    padded = (counts + tm - 1) // tm * tm
    start = jnp.cumsum(counts) - counts
    pend = jnp.cumsum(padded)
    pstart = pend - padded
    n_rows = -(-n_assign // tm) * tm + N_EXPERTS * tm
    n_blocks = n_rows // tm
    block_exp = jnp.minimum(jnp.searchsorted(pend, jnp.arange(n_blocks, dtype=jnp.int32) * tm, side="right"),
                            N_EXPERTS - 1).astype(jnp.int32)
    n_active = (pend[-1] // tm).astype(jnp.int32).reshape(1)
    row = jnp.arange(n_rows, dtype=jnp.int32)
    row_e = jnp.repeat(block_exp, tm)
    within = row - pstart[row_e]
    valid = jnp.logical_and(within < counts[row_e], row < pend[-1])
    src = order[jnp.clip(start[row_e] + within, 0, n_assign - 1)]
    buf_tok = jnp.where(valid, src // TOP_K, 0)
    buf_gate = jnp.where(valid, gate[src], 0.0)
    dest = (pstart[e_sorted] + jnp.arange(n_assign, dtype=jnp.int32) - start[e_sorted])[inv]
    return buf_tok, buf_gate.reshape(n_rows, 1), block_exp, n_active, dest.reshape(n_tok, TOP_K)


def _s5_tables(lam_re, lam_im, log_step, b_re, b_im, c_re, c_im, d_skip):
    lc = S5_CHUNK
    lam = lax.complex(lam_re.astype(F32), lam_im.astype(F32))
    step = jnp.exp(log_step.astype(F32))[..., None]
    lam_dt = lam * step
    lam_bar = jnp.exp(lam_dt)
    b_cplx = lax.complex(b_re.astype(F32), b_im.astype(F32))
    c_cplx = lax.complex(c_re.astype(F32), c_im.astype(F32))
    b_bar = ((lam_bar - 1.0) / lam)[..., None] * b_cplx[None]
    kk = jnp.arange(lc + 1, dtype=F32)
    pw = jnp.exp(lam_dt[..., None] * kk)
    kern = jnp.einsum("gcp,dgpk,dgpe->dgkce", c_cplx, pw[..., :lc], b_bar).real
    t_idx = jnp.arange(lc)
    lag = t_idx[:, None] - t_idx[None, :]
    kf = kern[0][:, jnp.clip(lag, 0, lc - 1)]
    kb = kern[1][:, jnp.clip(-lag, 0, lc - 1)]
    sel_f = (lag >= 0)[None, :, :, None, None]
    sel_b = (lag <= 0)[None, :, :, None, None]
    toep = jnp.where(sel_f, kf, 0.0) + jnp.where(sel_b, kb, 0.0)
    tmat = toep.transpose(0, 2, 4, 1, 3).reshape(S5_GROUPS, lc * S5_GROUP, lc * S5_GROUP)
    qf = jnp.einsum("gps,gpe->gsep", pw[0][..., :lc][..., ::-1], b_bar[0]).reshape(S5_GROUPS, lc * S5_GROUP, S5_STATE)
    qb = jnp.einsum("gps,gpe->gsep", pw[1][..., :lc], b_bar[1]).reshape(S5_GROUPS, lc * S5_GROUP, S5_STATE)
    pf = jnp.einsum("gcp,gpt->gptc", c_cplx, pw[0][..., 1:]).reshape(S5_GROUPS, S5_STATE, lc * S5_GROUP)
    pb = jnp.einsum("gcp,gpt->gptc", c_cplx, pw[1][..., 1:][..., ::-1]).reshape(S5_GROUPS, S5_STATE, lc * S5_GROUP)
    lam_chunk = pw[..., lc]
    npair = S5_GROUPS // 2
    gw = lc * S5_GROUP
    pair = lambda a: a.reshape((npair, 2) + a.shape[1:])
    tm2, qf2, qb2, pf2, pb2 = pair(tmat), pair(qf), pair(qb), pair(pf), pair(pb)
    t_pair = jnp.zeros((npair, 2 * gw, 2 * gw), F32)
    q_pair = jnp.zeros((npair, 2 * gw, 4 * LANE), F32)
    p_pair = jnp.zeros((npair, 4 * LANE, 2 * gw), F32)
    for k in range(2):
        rows = slice(k * gw, (k + 1) * gw)
        t_pair = t_pair.at[:, rows, rows].set(tm2[:, k])
        for d, (qq, pp) in enumerate(((qf2, pf2), (qb2, pb2))):
            c_re_ = slice(2 * d * LANE + k * S5_STATE, 2 * d * LANE + (k + 1) * S5_STATE)
            c_im_ = slice((2 * d + 1) * LANE + k * S5_STATE, (2 * d + 1) * LANE + (k + 1) * S5_STATE)
            q_pair = q_pair.at[:, rows, c_re_].set(qq[:, k].real)
            q_pair = q_pair.at[:, rows, c_im_].set(qq[:, k].imag)
            p_pair = p_pair.at[:, c_re_, rows].set(pp[:, k].real)
            p_pair = p_pair.at[:, c_im_, rows].set(-pp[:, k].imag)
    lam_c = lam_chunk.reshape(2, npair, 2 * S5_STATE)
    lam_rows = jnp.stack([lam_c[0].real, lam_c[0].imag, lam_c[1].real, lam_c[1].imag], axis=1)
    lam_rows = jnp.concatenate([lam_rows, jnp.zeros((npair, 4, LANE), F32)], axis=1)
    dsk = jnp.tile(d_skip.astype(F32).reshape(S5_GROUPS, 1, S5_GROUP), (1, lc, 1)).reshape(npair, 1, 2 * gw)
    return {"t": t_pair.astype(BF16), "q": q_pair.astype(BF16), "p": p_pair.astype(BF16), "lam": lam_rows, "dsk": dsk}


def _na_bias_table(rpb):
    cols = jnp.arange(GRID_W)
    col_start = jnp.clip(cols - NA_KW // 2, 0, GRID_W - NA_KW)
    in_win = (cols[None, :] >= col_start[:, None]) & (cols[None, :] < col_start[:, None] + NA_KW)
    dc = jnp.clip(cols[None, :] - cols[:, None], -(NA_KW - 1), NA_KW - 1) + NA_KW - 1
    e = jnp.arange(2 * NA_KH + 2)
    tabs = []
    for i in range(2):
        dr = e - 1 - i
        ok = (dr >= 0) & (dr <= 2 * NA_KH - 2)
        b = rpb.astype(F32)[:, jnp.clip(dr, 0, 2 * NA_KH - 2)][:, :, dc]
        b = jnp.where(ok[None, :, None, None] & in_win[None, None], b, NEG_BIG)
        tabs.append(b.transpose(0, 1, 3, 2))
    return jnp.concatenate(tabs, axis=-1)


def _rope_tables(n_ctx, t_len):
    pos = jnp.arange(t_len)
    row, col = pos // GRID_W, pos % GRID_W
    half = B_STATE // 2
    inv_freq = ROPE_THETA ** (-jnp.arange(0, half, 2, dtype=F32) / half)
    ang_r = row.astype(F32)[:, None] * inv_freq[None, :]
    ang_c = col.astype(F32)[:, None] * inv_freq[None, :]
    cos = jnp.concatenate([jnp.cos(ang_r)] * 2 + [jnp.cos(ang_c)] * 2, axis=-1)
    sin = jnp.concatenate([-jnp.sin(ang_r), jnp.sin(ang_r), -jnp.sin(ang_c), jnp.sin(ang_c)], axis=-1)
    cos = jnp.concatenate([jnp.ones((n_ctx, B_STATE), F32), cos], axis=0)
    sin = jnp.concatenate([jnp.zeros((n_ctx, B_STATE), F32), sin], axis=0)
    return cos, sin


def _ssd_params(conv_w, conv_b, a_log, dt_bias, d_skip, norm_g, cos, sin):
    a = -jnp.exp(a_log.astype(F32)).reshape(-1)
    pad = lambda v: jnp.concatenate([v, jnp.zeros((LANE - v.shape[0],), F32)])
    vec = jnp.zeros((8, LANE), F32).at[0].set(pad(dt_bias.astype(F32).reshape(-1))).at[1].set(pad(a))
    lanes = jnp.arange(LANE)[:, None]
    cols = jnp.arange(B_INNER)[None, :]
    expand = jnp.stack([(lanes == d * B_HEADS + cols // B_HEADDIM) for d in range(2)]).astype(BF16)
    sel = jnp.stack([jnp.broadcast_to(jnp.arange(LANE)[:, None] == j, (LANE, LANE)) for j in range(2 * B_HEADS)]).astype(BF16)
    cw = jnp.concatenate([conv_w.astype(F32), jnp.zeros((8 - B_CONV, B_CONV_DIM), F32)], axis=0)
    return {"conv_w": cw, "conv_b": conv_b.astype(F32).reshape(1, -1), "cos": cos, "sin": sin, "vec": vec,
            "expand": expand, "sel": sel, "dskip": jnp.repeat(d_skip.astype(F32), B_HEADDIM).reshape(1, -1),
            "norm": norm_g.astype(F32).reshape(1, -1)}


def _gla_params(lb, norm_g):
    lbh = lb.astype(F32).reshape(A_HEADS, A_DK)
    rows = jnp.stack([lbh, jnp.log(lbh), jnp.log1p(-lbh), 1.0 - lbh,
                      jnp.broadcast_to(norm_g.astype(F32), (A_HEADS, A_DK))], axis=1)
    return jnp.concatenate([rows, jnp.zeros((A_HEADS, 3, A_DK), F32)], axis=1)


def kernel(x, c, ctx, c_ctx, ada_w, ada_b, norm_mix, norm_ffn, w_in, hgrn_lb_logits, hgrn_norm, mamba_conv_w, mamba_conv_b, mamba_a_log, mamba_dt_bias, mamba_d, mamba_norm, na_rpb, s5_lam_re, s5_lam_im, s5_log_step, s5_b_re, s5_b_im, s5_c_re, s5_c_im, s5_d, s5_glu_w, s5_glu_b, proj_a, proj_b, proj_c, w_out, router_w, router_b, exp_w1, exp_b1, exp_w2, exp_b2, final_norm):
    bsz, t_len, dm = x.shape
    n_ctx = ctx.shape[1]
    s = n_ctx + t_len
    depth = w_in.shape[0]
    tr = math.gcd(n_ctx, 256)
    n_tok = bsz * s
    hp = lax.Precision.HIGHEST

    lb_all = jnp.cumsum(jax.nn.softmax(hgrn_lb_logits.astype(F32), axis=0), axis=0)
    lb_all = lb_all - lb_all[0:1]
    cond = jnp.concatenate([jax.nn.silu(c), jax.nn.silu(c_ctx)[None]], axis=0)
    cos, sin = _rope_tables(n_ctx, t_len)
    d_in = w_in.shape[-1]
    dt_lo = 4096

    xa = jnp.concatenate([ctx, x], axis=1)
    ymoe = None
    mod_prev = None
    for l in range(depth):
        m = (jnp.dot(cond, ada_w[l], precision=hp) + ada_b[l]).reshape(bsz + 1, 6, dm)
        mod = jnp.stack([jnp.broadcast_to(m[bsz], (bsz, 6, dm)), m[:bsz]], axis=1)
        w_l = w_in[l]
        w_re = jnp.concatenate([w_l[:, :dt_lo], w_l[:, dt_lo + 2 * B_HEADS:], w_l[:, dt_lo:dt_lo + 2 * B_HEADS],
                                jnp.zeros((dm, Z_WIDTH - d_in), w_l.dtype)], axis=1).astype(BF16)
        if ymoe is None:
            (h,) = _prep(xa, mod, norm_mix[l], n_ctx, tr)
        else:
            xa, h = _prep(xa, mod, norm_mix[l], n_ctx, tr, ymoe=ymoe, mod_prev=mod_prev)
        z = _in_proj(h.reshape(n_tok, dm), w_re).reshape(bsz, s, Z_WIDTH)

        a_out = _gla(z, _gla_params(lb_all[l], hgrn_norm[l]), n_ctx)
        b_out = _ssd(z, _ssd_params(mamba_conv_w[l], mamba_conv_b[l], mamba_a_log[l], mamba_dt_bias[l],
                                    mamba_d[l], mamba_norm[l], cos, sin), n_ctx)
        c_out = _na(z, _na_bias_table(na_rpb[l]), n_ctx)
        nj = s // S5_CHUNK
        u = z[:, :, CB_S5 * LANE:(CB_S5 + 4) * LANE].reshape(bsz, nj, S5_CHUNK, S5_GROUPS // 2, 2, S5_GROUP)
        u = u.transpose(3, 1, 0, 4, 2, 5).reshape(S5_GROUPS // 2, nj * bsz, 2 * S5_CHUNK * S5_GROUP)
        d_pairs = _s5(u, _s5_tables(s5_lam_re[l], s5_lam_im[l], s5_log_step[l], s5_b_re[l], s5_b_im[l],
                                    s5_c_re[l], s5_c_im[l], s5_d[l]), n_ctx, s, bsz)
        d_out = d_pairs.reshape(S5_GROUPS // 2, nj, bsz, 2, S5_CHUNK, S5_GROUP).transpose(2, 1, 4, 0, 3, 5)
        d_out = d_out.reshape(bsz, s, S5_WIDTH)

        rw = jnp.concatenate([router_w[l].astype(F32), jnp.zeros((dm, LANE - N_EXPERTS), F32)], axis=1)
        rw_hi = rw.astype(BF16)
        rw_lo = (rw - rw_hi.astype(F32)).astype(BF16)
        wts = {"proj_a": proj_a[l].astype(BF16), "proj_b": proj_b[l].astype(BF16), "proj_c": proj_c[l].astype(BF16),
               "glu_w": s5_glu_w[l].astype(BF16), "glu_b": s5_glu_b[l].astype(F32).reshape(1, -1),
               "w_out": w_out[l].astype(BF16), "norm_ffn": norm_ffn[l].astype(F32).reshape(1, -1),
               "router_w": jnp.stack([rw_hi, rw_lo]),
               "router_b": jnp.concatenate([router_b[l].astype(F32), jnp.zeros((LANE - N_EXPERTS,), F32)]).reshape(1, -1)}
        xa, h2, logits = _merge(a_out, b_out, c_out, d_out, z, xa, mod, wts, n_ctx, tr)

        buf_tok, buf_gate, block_exp, n_active, dest = _route(logits.reshape(n_tok, LANE)[:, :N_EXPERTS], n_tok)
        hs = jnp.take(h2.reshape(n_tok, dm), buf_tok, axis=0)
        ew = {"w1g": exp_w1[l][:, :, 0::2].astype(BF16), "w1l": exp_w1[l][:, :, 1::2].astype(BF16),
              "b1g": exp_b1[l][:, None, 0::2].astype(F32), "b1l": exp_b1[l][:, None, 1::2].astype(F32),
              "w2": exp_w2[l].astype(BF16), "b2": exp_b2[l][:, None, :].astype(F32)}
        yb = _moe_ffn(hs, buf_gate, block_exp, n_active, ew)
        ymoe = jnp.take(yb, dest.T, axis=0).reshape(TOP_K, bsz, s, dm)
        mod_prev = mod

    zero_mod = jnp.zeros_like(mod_prev)
    (out,) = _prep(xa, zero_mod, final_norm, n_ctx, tr, ymoe=ymoe, mod_prev=mod_prev, final=True)
    return out
```

```python
import functools
import math

import numpy as np
import jax
import jax.numpy as jnp
from jax import lax
from jax.experimental import pallas as pl
from jax.experimental.pallas import tpu as pltpu

F32 = jnp.float32
BF16 = jnp.bfloat16

D_MODEL = 1024
GRID_W = 64
EPS = 1e-6
N_BRANCH = 4
A_HEADS = 4
A_DK = 128
A_WIDTH = 512
B_HEADS = 8
B_HEADDIM = 64
B_INNER = 512
B_GROUPS = 2
B_STATE = 128
B_CONV = 5
B_CONV_DIM = 1024
ROPE_THETA = 10000.0
C_HEADS = 8
C_HEADDIM = 64
C_WIDTH = 512
NA_KH = 8
NA_KW = 16
S5_GROUP = 16
S5_GROUPS = 32
S5_WIDTH = 512
S5_STATE = 64
N_EXPERTS = 32
TOP_K = 4
D_EXPERT = 1024
SWIGLU_LIMIT = 7.0
SWIGLU_ALPHA = 1.702

LANE = 128
GLA_BLK = 16
CHUNK = 128
S5_CHUNK = 16
NA_WIN = 10
MOE_TM = 512
NEG_BIG = -1e30
VMEM_LIMIT = 56 * 1024 * 1024

CB_Q, CB_FF, CB_FB, CB_I, CB_G = 0, 4, 8, 12, 16
CB_MZ, CB_XBC, CB_QKV, CB_S5, CB_GATE, CB_DT = 20, 24, 32, 44, 48, 80
Z_WIDTH = 81 * LANE
Z_TN = 9 * LANE


def _dot(a, b):
    return jnp.dot(a, b, preferred_element_type=F32)


def _dot_nt(a, b):
    return lax.dot_general(a, b, (((1,), (1,)), ((), ())), preferred_element_type=F32)


def _hi_lo(x):
    hi = x.astype(BF16)
    lo = (x - hi.astype(F32)).astype(BF16)
    return hi, lo


def _sel_dot(mat, x):
    hi, lo = _hi_lo(x)
    return _dot(mat, hi) + _dot(mat, lo)


def _dot_sel(x, mat):
    hi, lo = _hi_lo(x)
    return _dot(hi, mat) + _dot(lo, mat)


def _cparams(sem):
    return pltpu.CompilerParams(dimension_semantics=sem, vmem_limit_bytes=VMEM_LIMIT)


def _prep_kernel(*refs, has_moe, final):
    if has_moe:
        x_ref, y_ref, modp_ref, mod_ref, g_ref = refs[:5]
        outs = refs[5:]
    else:
        x_ref, mod_ref, g_ref = refs[:3]
        outs = refs[3:]
    x = x_ref[0]
    if has_moe:
        f = (y_ref[0, 0].astype(F32) + y_ref[1, 0].astype(F32)) + (y_ref[2, 0].astype(F32) + y_ref[3, 0].astype(F32))
        x = x + modp_ref[0, 0, 5:6, :] * f
    y = x * lax.rsqrt(jnp.mean(x * x, axis=-1, keepdims=True) + EPS) * g_ref[...]
    if final:
        outs[0][0] = y
        return
    h = y * (1.0 + mod_ref[0, 0, 1:2, :]) + mod_ref[0, 0, 0:1, :]
    if has_moe:
        outs[0][0] = x
        outs[1][0] = h.astype(BF16)
    else:
        outs[0][0] = h.astype(BF16)


def _prep(x, mod, g, n_ctx, tr, ymoe=None, mod_prev=None, final=False):
    bsz, s, d = x.shape
    nct = n_ctx // tr
    has_moe = ymoe is not None
    kind = lambda j: jnp.minimum(j // nct, 1)
    if final:
        grid = (bsz, (s - n_ctx) // tr)
        off = nct
    else:
        grid = (bsz, s // tr)
        off = 0
    x_spec = pl.BlockSpec((1, tr, d), lambda b, j: (b, j + off, 0))
    mod_spec = pl.BlockSpec((1, 1, 6, d), lambda b, j: (b, kind(j + off), 0, 0))
    g_spec = pl.BlockSpec((1, d), lambda b, j: (0, 0))
    in_specs, args = [x_spec], [x]
    if has_moe:
        in_specs += [pl.BlockSpec((4, 1, tr, d), lambda b, j: (0, b, j + off, 0)), mod_spec]
        args += [ymoe, mod_prev]
    in_specs += [mod_spec, g_spec]
    args += [mod, g.reshape(1, d)]
    if final:
        out_shape = [jax.ShapeDtypeStruct((bsz, s - n_ctx, d), F32)]
        out_specs = [pl.BlockSpec((1, tr, d), lambda b, j: (b, j, 0))]
    elif has_moe:
        out_shape = [jax.ShapeDtypeStruct((bsz, s, d), F32), jax.ShapeDtypeStruct((bsz, s, d), BF16)]
        out_specs = [x_spec, x_spec]
    else:
        out_shape = [jax.ShapeDtypeStruct((bsz, s, d), BF16)]
        out_specs = [x_spec]
    return pl.pallas_call(
        functools.partial(_prep_kernel, has_moe=has_moe, final=final), name="prep_norm",
        grid=grid, in_specs=in_specs, out_specs=out_specs, out_shape=out_shape,
        compiler_params=_cparams(("parallel", "parallel")),
    )(*args)


def _matmul_kernel(h_ref, w_ref, z_ref):
    z_ref[...] = _dot(h_ref[...], w_ref[...]).astype(z_ref.dtype)


def _in_proj(h2d, w):
    m, k = h2d.shape
    n = w.shape[1]
    tm = next(t for t in (1024, 512, 256, 128) if m % t == 0)
    return pl.pallas_call(
        _matmul_kernel, name="in_proj",
        grid=(n // Z_TN, m // tm),
        in_specs=[pl.BlockSpec((tm, k), lambda j, i: (i, 0)), pl.BlockSpec((k, Z_TN), lambda j, i: (0, j))],
        out_specs=pl.BlockSpec((tm, Z_TN), lambda j, i: (i, j)),
        out_shape=jax.ShapeDtypeStruct((m, n), BF16),
        compiler_params=_cparams(("parallel", "parallel")),
    )(h2d, w)


def _gla_kernel(q_ref, ff_ref, fb_ref, i_ref, g_ref, p_ref, o_ref,
                b_scr, k_scr, v_scr, qt_scr, dec_scr, od_scr, x_scr, *, n_ctx, s_len):
    nchunk = s_len // CHUNK
    nblk = s_len // GLA_BLK
    nb_ctx = n_ctx // GLA_BLK
    bpc = CHUNK // GLA_BLK
    scale = A_DK ** -0.5
    log_lb = p_ref[0, 1:2, :]
    log_1mlb = p_ref[0, 2:3, :]
    omlb = p_ref[0, 3:4, :]
    ng = p_ref[0, 4:5, :]
    ri = lax.broadcasted_iota(jnp.int32, (CHUNK, CHUNK), 0)
    ci = lax.broadcasted_iota(jnp.int32, (CHUNK, CHUNK), 1)
    same = (ri // GLA_BLK) == (ci // GLA_BLK)
    tri = (jnp.where(same, jnp.where(ci <= ri, 1.0, 0.0), 0.0).astype(BF16),
           jnp.where(same, jnp.where(ci >= ri, 1.0, 0.0), 0.0).astype(BF16))
    blk_sum = jnp.where(same, 1.0, 0.0).astype(BF16)
    cblk = ci // GLA_BLK

    def bulk(c, _):
        r0 = pl.multiple_of(c * CHUNK, CHUNK)
        q = q_ref[0, pl.ds(r0, CHUNK), :].astype(F32) * scale
        v = i_ref[0, pl.ds(r0, CHUNK), :].astype(F32)
        v_scr[pl.ds(r0, CHUNK), :] = v
        v_t = v.T
        for d, fz_ref in enumerate((ff_ref, fb_ref)):
            fz = fz_ref[0, pl.ds(r0, CHUNK), :].astype(F32)
            e = jnp.exp(-jnp.abs(fz))
            log_sig = jnp.minimum(fz, 0.0) - jnp.log1p(e)
            k = omlb * (jnp.where(fz >= 0, e, 1.0) / (1.0 + e))
            t2 = log_1mlb + log_sig
            lf = jnp.maximum(log_lb, t2) + jnp.log1p(jnp.exp(-jnp.abs(log_lb - t2)))
            hi, lo = _hi_lo(lf)
            bcum = _dot(tri[d], hi) + _dot(tri[d], lo)
            tot = _dot(blk_sum, hi) + _dot(blk_sum, lo)
            b_scr[d, pl.ds(r0, CHUNK), :] = bcum
            k_scr[d, pl.ds(r0, CHUNK), :] = k
            qt_scr[d, pl.ds(r0, CHUNK), :] = (q * jnp.exp(bcum)).astype(BF16)
            dec_scr[d, pl.ds(r0, CHUNK), :] = jnp.exp(tot)
            kt = (k * jnp.exp(tot - bcum)).astype(BF16)
            for j in range(bpc):
                x_scr[d, c * bpc + j] = _dot(jnp.where(cblk == j, v_t, 0.0).astype(BF16), kt).astype(BF16)
        return 0

    lax.fori_loop(0, nchunk, bulk, 0)

    tio = lax.broadcasted_iota(jnp.int32, (GLA_BLK, LANE), 0)
    ones_red = jnp.ones((LANE, LANE), BF16)

    def diag(i, _):
        r0 = pl.multiple_of(i * GLA_BLK, GLA_BLK)
        qq = q_ref[0, pl.ds(r0, GLA_BLK), :].astype(F32) * scale
        for d in range(2):
            bb = b_scr[d, pl.ds(r0, GLA_BLK), :]
            pieces = []
            for s in range(GLA_BLK):
                bs = b_scr[d, pl.ds(r0 + s, 1), :]
                ks = k_scr[d, pl.ds(r0 + s, 1), :]
                msk = (tio >= s) if d == 0 else (tio <= s)
                w = jnp.exp(jnp.where(msk, bb - bs, -jnp.inf))
                pieces.append(((qq * w) * ks).astype(BF16))
            red = _dot(jnp.concatenate(pieces, axis=0), ones_red)
            od = jnp.zeros((GLA_BLK, LANE), F32)
            for s in range(GLA_BLK):
                od = od + red[s * GLA_BLK:(s + 1) * GLA_BLK, :] * v_scr[pl.ds(r0 + s, 1), :]
            od_scr[d, pl.ds(r0, GLA_BLK), :] = od
        return 0

    lax.fori_loop(0, nblk, diag, 0, unroll=2)

    def scan(i, carry):
        new = []
        order = (i, jnp.where(i < nb_ctx, nb_ctx - 1 - i, nblk - 1 - (i - nb_ctx)))
        for d in range(2):
            j = order[d]
            st = carry[d]
            ds_blk = x_scr[d, j].astype(F32)
            x_scr[d, j] = st.astype(BF16)
            new.append(st * dec_scr[d, pl.ds(j * GLA_BLK, 1), :] + ds_blk)
        return tuple(new)

    zero_state = jnp.zeros((LANE, LANE), F32)
    lax.fori_loop(0, nblk, scan, (zero_state, zero_state))

    def emit(c, _):
        r0 = pl.multiple_of(c * CHUNK, CHUNK)
        parts = []
        for j in range(bpc):
            rows = pl.ds(r0 + j * GLA_BLK, GLA_BLK)
            o = od_scr[0, rows, :] + od_scr[1, rows, :]
            o = o + _dot_nt(qt_scr[0, rows, :], x_scr[0, c * bpc + j])
            parts.append(o + _dot_nt(qt_scr[1, rows, :], x_scr[1, c * bpc + j]))
        o = jnp.concatenate(parts, axis=0)
        o = o * lax.rsqrt(jnp.mean(o * o, axis=-1, keepdims=True) + EPS) * ng
        g = g_ref[0, pl.ds(r0, CHUNK), :].astype(F32)
        o_ref[0, pl.ds(r0, CHUNK), :] = (o * (g * jax.nn.sigmoid(g))).astype(BF16)
        return 0

    lax.fori_loop(0, nchunk, emit, 0)


def _gla(z, params, n_ctx):
    bsz, s, _ = z.shape
    spec = lambda cb: pl.BlockSpec((1, s, LANE), lambda b, h: (b, 0, cb + h))
    return pl.pallas_call(
        functools.partial(_gla_kernel, n_ctx=n_ctx, s_len=s), name="gla_mixer",
        grid=(bsz, A_HEADS),
        in_specs=[spec(CB_Q), spec(CB_FF), spec(CB_FB), spec(CB_I), spec(CB_G),
                  pl.BlockSpec((1, 8, LANE), lambda b, h: (h, 0, 0))],
        out_specs=pl.BlockSpec((1, s, LANE), lambda b, h: (b, 0, h)),
        out_shape=jax.ShapeDtypeStruct((bsz, s, A_WIDTH), BF16),
        scratch_shapes=[
            pltpu.VMEM((2, s, LANE), F32), pltpu.VMEM((2, s, LANE), F32), pltpu.VMEM((s, LANE), F32),
            pltpu.VMEM((2, s, LANE), BF16), pltpu.VMEM((2, s, LANE), F32), pltpu.VMEM((2, s, LANE), F32),
            pltpu.VMEM((2, s // GLA_BLK, LANE, LANE), BF16),
        ],
        compiler_params=_cparams(("parallel", "parallel")),
    )(z, z, z, z, z, params)


def _ssd_kernel(mz_ref, xbc_ref, dt_ref, cw_ref, cb_ref, cos_ref, sin_ref, vec_ref, exp_ref, sel_ref,
                dsk_ref, ng_ref, o_ref, xs_scr, bm_scr, cm_scr, dt_scr, y_scr, *, n_ctx, s_len):
    nchunk = s_len // CHUNK
    nc_ctx = n_ctx // CHUNK
    halo = 16
    ri = lax.broadcasted_iota(jnp.int32, (CHUNK, CHUNK), 0)
    ci = lax.broadcasted_iota(jnp.int32, (CHUNK, CHUNK), 1)
    lane = lax.broadcasted_iota(jnp.int32, (1, LANE), 1)
    dt_bias = vec_ref[0:1, :]
    a_lane = vec_ref[1:2, :]
    dt_valid = lane < 2 * B_HEADS

    wi_r = lax.broadcasted_iota(jnp.int32, (CHUNK, CHUNK + 2 * halo), 0)
    wi_c = lax.broadcasted_iota(jnp.int32, (CHUNK, CHUNK + 2 * halo), 1)
    shift = [jnp.where(wi_c == wi_r + halo + o, 1.0, 0.0).astype(BF16) for o in range(-(B_CONV // 2), B_CONV // 2 + 1)]

    def prep(c, _):
        r0 = pl.multiple_of(c * CHUNK, CHUNK)
        same_prev = jnp.logical_and(c > 0, c != nc_ctx)
        same_next = jnp.logical_and(c < nchunk - 1, c != nc_ctx - 1)
        rp = pl.multiple_of(jnp.maximum(r0 - halo, 0), halo)
        rn = pl.multiple_of(jnp.minimum(r0 + CHUNK, s_len - halo), halo)
        prev = jnp.where(same_prev, xbc_ref[0, pl.ds(rp, halo), :].astype(F32), 0.0).astype(BF16)
        nxt = jnp.where(same_next, xbc_ref[0, pl.ds(rn, halo), :].astype(F32), 0.0).astype(BF16)
        win = jnp.concatenate([prev, xbc_ref[0, pl.ds(r0, CHUNK), :], nxt], axis=0)
        u = jnp.zeros((CHUNK, B_CONV_DIM), F32) + cb_ref[...]
        for j in range(B_CONV):
            u = u + _dot(shift[j], win) * cw_ref[j:j + 1, :]
        u = u * jax.nn.sigmoid(u)
        xs_scr[pl.ds(r0, CHUNK), :] = u[:, :B_INNER].astype(BF16)
        cos = cos_ref[pl.ds(r0, CHUNK), :]
        sin = sin_ref[pl.ds(r0, CHUNK), :]
        first = (lane % 64) < 32
        for k, scr in ((0, bm_scr), (1, cm_scr)):
            for g in range(B_GROUPS):
                lo = B_INNER + (k * B_GROUPS + g) * B_STATE
                vv = u[:, lo:lo + B_STATE]
                partner = jnp.where(first, pltpu.roll(vv, LANE - 32, axis=1), pltpu.roll(vv, 32, axis=1))
                scr[pl.ds(r0, CHUNK), g * B_STATE:(g + 1) * B_STATE] = (vv * cos + partner * sin).astype(BF16)
        xd = dt_ref[0, pl.ds(r0, CHUNK), :].astype(F32) + dt_bias
        dt = jnp.maximum(xd, 0.0) + jnp.log1p(jnp.exp(-jnp.abs(xd)))
        dt_scr[pl.ds(r0, CHUNK), :] = jnp.where(dt_valid, dt, 0.0)
        y_scr[pl.ds(r0, CHUNK), :] = u[:, :B_INNER] * dsk_ref[...]
        return 0

    lax.fori_loop(0, nchunk, prep, 0)

    tri = (jnp.where(ci <= ri, 1.0, 0.0).astype(BF16), jnp.where(ci >= ri, 1.0, 0.0).astype(BF16))
    keep = (ci <= ri, ci >= ri)
    lane_lo = lax.broadcasted_iota(jnp.int32, (1, LANE), 1) < B_HEADDIM
    gw = B_INNER // B_GROUPS

    def step(i, carry):
        new = []
        order = (i, jnp.where(i < nc_ctx, nc_ctx - 1 - i, nchunk - 1 - (i - nc_ctx)))
        for d in range(2):
            c = order[d]
            r0 = pl.multiple_of(c * CHUNK, CHUNK)
            st = carry[d]
            dt = dt_scr[pl.ds(r0, CHUNK), :]
            acum = _sel_dot(tri[d], dt * a_lane)
            acum_t = acum.T
            expand = exp_ref[d]
            xdt = xs_scr[pl.ds(r0, CHUNK), :].astype(F32) * _dot_sel(dt, expand)
            ea_x = _dot_sel(jnp.exp(acum), expand)
            last = CHUNK - 1 if d == 0 else 0
            total = acum[last:last + 1, :]
            w_x = _dot_sel(jnp.exp(total - acum), expand)
            dec_row = ea_x[last:last + 1, :]
            xw_b = (xdt * w_x).astype(BF16)
            hi, lo = _hi_lo(acum)
            y_parts = []
            s_parts = []
            for g in range(B_GROUPS):
                bm = bm_scr[pl.ds(r0, CHUNK), g * B_STATE:(g + 1) * B_STATE]
                cm = cm_scr[pl.ds(r0, CHUNK), g * B_STATE:(g + 1) * B_STATE]
                gmat = _dot_nt(cm, bm)
                y_off = _dot(cm, st[:, g * gw:(g + 1) * gw].astype(BF16))
                pair_out = []
                for pp in range(2):
                    p = g * 2 + pp
                    xp = xdt[:, p * LANE:(p + 1) * LANE]
                    acc = jnp.zeros((CHUNK, LANE), F32)
                    for hh in range(2):
                        j = d * B_HEADS + 2 * p + hh
                        col = _dot(hi, sel_ref[j]) + _dot(lo, sel_ref[j])
                        seg = col - acum_t[j:j + 1, :]
                        m = (gmat * jnp.exp(jnp.where(keep[d], seg, -jnp.inf))).astype(BF16)
                        xm = jnp.where(lane_lo, xp, 0.0) if hh == 0 else jnp.where(lane_lo, 0.0, xp)
                        acc = acc + _dot(m, xm.astype(BF16))
                    pair_out.append(acc)
                y_parts.append(jnp.concatenate(pair_out, axis=1) + y_off * ea_x[:, g * gw:(g + 1) * gw])
                bm_t = bm.astype(F32).T.astype(BF16)
                s_parts.append(_dot(bm_t, xw_b[:, g * gw:(g + 1) * gw]))
            y_scr[pl.ds(r0, CHUNK), :] += jnp.concatenate(y_parts, axis=1)
            new.append(st * dec_row + jnp.concatenate(s_parts, axis=1))
        return tuple(new)

    zero_state = jnp.zeros((B_STATE, B_INNER), F32)
    lax.fori_loop(0, nchunk, step, (zero_state, zero_state))

    def finish(c, _):
        r0 = pl.multiple_of(c * CHUNK, CHUNK)
        zz = mz_ref[0, pl.ds(r0, CHUNK), :].astype(F32)
        y = y_scr[pl.ds(r0, CHUNK), :] * (zz * jax.nn.sigmoid(zz))
        y = y * lax.rsqrt(jnp.mean(y * y, axis=-1, keepdims=True) + EPS) * ng_ref[...]
        o_ref[0, pl.ds(r0, CHUNK), :] = y.astype(BF16)
        return 0

    lax.fori_loop(0, nchunk, finish, 0)


def _ssd(z, p, n_ctx):
    bsz, s, _ = z.shape
    const = lambda shape: pl.BlockSpec(shape, lambda b: (0,) * len(shape))
    return pl.pallas_call(
        functools.partial(_ssd_kernel, n_ctx=n_ctx, s_len=s), name="ssd_mixer",
        grid=(bsz,),
        in_specs=[
            pl.BlockSpec((1, s, B_INNER), lambda b: (b, 0, CB_MZ * LANE // B_INNER)),
            pl.BlockSpec((1, s, B_CONV_DIM), lambda b: (b, 0, CB_XBC * LANE // B_CONV_DIM)),
            pl.BlockSpec((1, s, LANE), lambda b: (b, 0, CB_DT)),
            const((8, B_CONV_DIM)), const((1, B_CONV_DIM)), const((s, LANE)), const((s, LANE)),
            const((8, LANE)), const((2, LANE, B_INNER)), const((2 * B_HEADS, LANE, LANE)),
            const((1, B_INNER)), const((1, B_INNER)),
        ],
        out_specs=pl.BlockSpec((1, s, B_INNER), lambda b: (b, 0, 0)),
        out_shape=jax.ShapeDtypeStruct((bsz, s, B_INNER), BF16),
        scratch_shapes=[
            pltpu.VMEM((s, B_INNER), BF16), pltpu.VMEM((s, B_GROUPS * B_STATE), BF16),
            pltpu.VMEM((s, B_GROUPS * B_STATE), BF16), pltpu.VMEM((s, LANE), F32), pltpu.VMEM((s, B_INNER), F32),
        ],
        compiler_params=_cparams(("parallel",)),
    )(z, z, z, p["conv_w"], p["conv_b"], p["cos"], p["sin"], p["vec"], p["expand"], p["sel"], p["dskip"], p["norm"])


def _na_kernel(q_ref, k_ref, v_ref, bias_ref, o_ref, vt_scr, *, n_ctx, s_len):
    rows = (s_len - n_ctx) // GRID_W
    n_pairs = rows // 2
    n_vt = s_len // LANE
    nct = n_ctx // LANE
    win = NA_WIN * GRID_W
    scale = C_HEADDIM ** -0.5
    lane = lax.broadcasted_iota(jnp.int32, (1, LANE), 1)
    head_lanes = (lane < C_HEADDIM, lane >= C_HEADDIM)
    sub_lo = lax.broadcasted_iota(jnp.int32, (LANE, 1), 0) < C_HEADDIM

    def transpose_v(t, _):
        r0 = pl.multiple_of(t * LANE, LANE)
        vt_scr[t] = v_ref[0, pl.ds(r0, LANE), :].astype(F32).T.astype(BF16)
        return 0

    lax.fori_loop(0, n_vt, transpose_v, 0)

    k_ctx = k_ref[0, 0:n_ctx, :]

    qc = q_ref[0, 0:n_ctx, :].astype(F32) * scale
    v_ctx = v_ref[0, 0:n_ctx, :]
    out_c = jnp.zeros((n_ctx, LANE), F32)
    for hh in range(2):
        qm = jnp.where(head_lanes[hh], qc, 0.0).astype(BF16)
        sc = _dot_nt(qm, k_ctx)
        pc = jnp.exp(sc - jnp.max(sc, axis=-1, keepdims=True))
        oc = _dot(pc.astype(BF16), v_ctx) / jnp.sum(pc, axis=-1, keepdims=True)
        out_c = jnp.where(head_lanes[hh], oc, out_c)
    o_ref[0, 0:n_ctx, :] = out_c.astype(BF16)

    def pair(rp, _):
        r0 = 2 * rp
        su = jnp.clip(r0 - NA_KH // 2, 0, rows - NA_WIN)
        q0 = pl.multiple_of(n_ctx + r0 * GRID_W, LANE)
        k0 = pl.multiple_of(n_ctx + su * GRID_W, LANE)
        qp = q_ref[0, pl.ds(q0, LANE), :].astype(F32) * scale
        kw = k_ref[0, pl.ds(k0, win), :]
        vt0 = nct + su // 2
        starts = [jnp.clip(r0 + i - NA_KH // 2, 0, rows - NA_KH) for i in range(2)]
        out_t = []
        for hh in range(2):
            qm = jnp.where(head_lanes[hh], qp, 0.0).astype(BF16)
            st = _dot_nt(kw, qm)
            sc = _dot_nt(k_ctx, qm)
            tiles = []
            for w in range(NA_WIN):
                kr = su + w
                ok = [jnp.logical_and(kr >= starts[i], kr < starts[i] + NA_KH).astype(jnp.int32) for i in range(2)]
                okv = jnp.where(lane < GRID_W, ok[0], ok[1]) > 0
                bias = bias_ref[hh, kr - r0 + NA_KH]
                tiles.append(jnp.where(okv, st[w * GRID_W:(w + 1) * GRID_W, :] + bias, NEG_BIG))
            sw = jnp.concatenate(tiles, axis=0)
            m = jnp.maximum(jnp.max(sw, axis=0, keepdims=True), jnp.max(sc, axis=0, keepdims=True))
            pw = jnp.exp(sw - m)
            pc = jnp.exp(sc - m)
            den = jnp.sum(pw, axis=0, keepdims=True) + jnp.sum(pc, axis=0, keepdims=True)
            pw = pw.astype(BF16)
            pc = pc.astype(BF16)
            acc = jnp.zeros((LANE, LANE), F32)
            for t in range(win // LANE):
                acc = acc + _dot(vt_scr[vt0 + t], pw[t * LANE:(t + 1) * LANE, :])
            for t in range(nct):
                acc = acc + _dot(vt_scr[t], pc[t * LANE:(t + 1) * LANE, :])
            out_t.append(acc / den)
        o_ref[0, pl.ds(q0, LANE), :] = jnp.where(sub_lo, out_t[0], out_t[1]).T.astype(BF16)
        return 0

    lax.fori_loop(0, n_pairs, pair, 0)


def _na(z, bias, n_ctx):
    bsz, s, _ = z.shape
    spec = lambda cb: pl.BlockSpec((1, s, LANE), lambda b, p: (b, 0, cb + p))
    return pl.pallas_call(
        functools.partial(_na_kernel, n_ctx=n_ctx, s_len=s), name="na_mixer",
        grid=(bsz, C_HEADS // 2),
        in_specs=[spec(CB_QKV), spec(CB_QKV + 4), spec(CB_QKV + 8),
                  pl.BlockSpec((2, 2 * NA_KH + 2, GRID_W, LANE), lambda b, p: (p, 0, 0, 0))],
        out_specs=pl.BlockSpec((1, s, LANE), lambda b, p: (b, 0, p)),
        out_shape=jax.ShapeDtypeStruct((bsz, s, C_WIDTH), BF16),
        scratch_shapes=[pltpu.VMEM((s // LANE, LANE, LANE), BF16)],
        compiler_params=_cparams(("parallel", "parallel")),
    )(z, z, z, bias)


def _s5_pack_kernel(u_ref, place_ref, r_ref, x_scr, *, nj):
    x_scr[...] = u_ref[0].astype(F32)
    xs = [x_scr[pl.ds(s, nj, stride=S5_CHUNK), :].astype(BF16) for s in range(S5_CHUNK)]
    for a in range(4):
        for m in range(4):
            acc = jnp.zeros((nj, LANE), F32)
            for i in range(4):
                acc = acc + _dot(xs[4 * m + i], place_ref[a, i])
            r_ref[a, 0, :, m * LANE:(m + 1) * LANE] = acc.astype(BF16)


def _s5_unpack_kernel(y_ref, place_ref, o_ref, o_scr, *, nj):
    for t in range(S5_CHUNK):
        acc = jnp.zeros((nj, LANE), F32)
        for a in range(4):
            acc = acc + _dot(y_ref[a, 0, :, (t // 4) * LANE:(t // 4 + 1) * LANE], place_ref[t % 4, a])
        o_scr[pl.ds(t, nj, stride=S5_CHUNK), :] = acc
    o_ref[0] = o_scr[...].astype(BF16)


def _s5_kernel(u_ref, t_ref, q_ref, p_ref, lam_ref, dsk_ref, o_ref, w_scr, z_scr, *, n_ctx, s_len, bsz):
    nj = s_len // S5_CHUNK
    nj_ctx = n_ctx // S5_CHUNK
    qmat = q_ref[0]

    def project(b, _):
        r0 = pl.multiple_of(b * nj, 8)
        w = _dot(u_ref[0, b], qmat)
        for k in range(4):
            w_scr[k, pl.ds(r0, nj), :] = w[:, k * LANE:(k + 1) * LANE]
        return 0

    lax.fori_loop(0, bsz, project, 0)

    lam = [lam_ref[0, k:k + 1, :] for k in range(4)]

    def scan(i, carry):
        order = (i, jnp.where(i < nj_ctx, nj_ctx - 1 - i, nj - 1 - (i - nj_ctx)))
        new = []
        for d in range(2):
            rows = pl.ds(order[d], bsz, stride=nj)
            z_re, z_im = carry[2 * d], carry[2 * d + 1]
            a_re, a_im = lam[2 * d], lam[2 * d + 1]
            z_scr[2 * d, rows, :] = z_re
            z_scr[2 * d + 1, rows, :] = z_im
            w_re = w_scr[2 * d, rows, :]
            w_im = w_scr[2 * d + 1, rows, :]
            new.append(a_re * z_re - a_im * z_im + w_re)
            new.append(a_re * z_im + a_im * z_re + w_im)
        return tuple(new)

    z0 = jnp.zeros((bsz, LANE), F32)
    lax.fori_loop(0, nj, scan, (z0, z0, z0, z0))

    tmat = t_ref[0]
    pmat = p_ref[0]
    dsk = dsk_ref[0]

    def emit(b, _):
        r0 = pl.multiple_of(b * nj, 8)
        u = u_ref[0, b]
        zp = jnp.concatenate([z_scr[k, pl.ds(r0, nj), :].astype(BF16) for k in range(4)], axis=1)
        y = _dot(u, tmat) + _dot(zp, pmat) + dsk * u.astype(F32)
        y = 0.5 * y * (1.0 + jnp.tanh(math.sqrt(2.0 / math.pi) * (y + 0.044715 * (y * y * y))))
        o_ref[0, b] = y.astype(BF16)
        return 0

    lax.fori_loop(0, bsz, emit, 0)


def _s5(z, p, n_ctx):
    bsz, s, _ = z.shape
    nj = s // S5_CHUNK
    npair = S5_GROUPS // 2
    width = 2 * S5_CHUNK * S5_GROUP
    packed_spec = pl.BlockSpec((4, 1, nj, width), lambda b, q: (q, b, 0, 0))
    place_spec = pl.BlockSpec((4, 4, LANE, LANE), lambda b, q: (0, 0, 0, 0))
    packed_shape = jax.ShapeDtypeStruct((npair, bsz, nj, width), BF16)
    u = pl.pallas_call(
        functools.partial(_s5_pack_kernel, nj=nj), name="s5_pack",
        grid=(bsz, 4),
        in_specs=[pl.BlockSpec((1, s, LANE), lambda b, q: (b, 0, CB_S5 + q)), place_spec],
        out_specs=packed_spec, out_shape=packed_shape,
        scratch_shapes=[pltpu.VMEM((s, LANE), F32)],
        compiler_params=_cparams(("parallel", "parallel")),
    )(z, p["place"])
    blk = lambda shape: pl.BlockSpec((1,) + shape, lambda g: (g,) + (0,) * len(shape))
    y = pl.pallas_call(
        functools.partial(_s5_kernel, n_ctx=n_ctx, s_len=s, bsz=bsz), name="s5_core",
        grid=(npair,),
        in_specs=[blk((bsz, nj, width)), blk((width, width)), blk((width, width)), blk((width, width)),
                  blk((8, LANE)), blk((1, width))],
        out_specs=blk((bsz, nj, width)), out_shape=packed_shape,
        scratch_shapes=[pltpu.VMEM((4, bsz * nj, LANE), F32), pltpu.VMEM((4, bsz * nj, LANE), F32)],
        compiler_params=_cparams(("parallel",)),
    )(u, p["t"], p["q"], p["p"], p["lam"], p["dsk"])
    return pl.pallas_call(
        functools.partial(_s5_unpack_kernel, nj=nj), name="s5_unpack",
        grid=(bsz, 4),
        in_specs=[packed_spec, place_spec],
        out_specs=pl.BlockSpec((1, s, LANE), lambda b, q: (b, 0, q)),
        out_shape=jax.ShapeDtypeStruct((bsz, s, S5_WIDTH), BF16),
        scratch_shapes=[pltpu.VMEM((s, LANE), F32)],
        compiler_params=_cparams(("parallel", "parallel")),
    )(y, p["place"])


def _merge_kernel(a_ref, b_ref, c_ref, d_ref, g0_ref, g1_ref, g2_ref, g3_ref, x_ref, mod_ref,
                  pa_ref, pb_ref, pc_ref, gw_ref, gb_ref, wo_ref, nf_ref, rw_ref, rb_ref,
                  xo_ref, h_ref, lg_ref):
    sig = lambda r: jax.nn.sigmoid(r[0].astype(F32))
    glu = _dot(d_ref[0], gw_ref[...]) + gb_ref[...]
    yd = glu[:, :D_MODEL] * jax.nn.sigmoid(glu[:, D_MODEL:])
    y = sig(g0_ref) * _dot(a_ref[0], pa_ref[...]) + sig(g1_ref) * _dot(b_ref[0], pb_ref[...])
    y = y + sig(g2_ref) * _dot(c_ref[0], pc_ref[...]) + sig(g3_ref) * yd
    y = _dot(y.astype(BF16), wo_ref[...])
    x = x_ref[0] + mod_ref[0, 0, 2:3, :] * y
    xo_ref[0] = x
    h = x * lax.rsqrt(jnp.mean(x * x, axis=-1, keepdims=True) + EPS) * nf_ref[...]
    h = h * (1.0 + mod_ref[0, 0, 4:5, :]) + mod_ref[0, 0, 3:4, :]
    h_ref[0] = h.astype(BF16)
    hi, lo = _hi_lo(h)
    lg_ref[0] = _dot(hi, rw_ref[0]) + _dot(lo, rw_ref[0]) + _dot(hi, rw_ref[1]) + rb_ref[...]


def _merge(a, b, c, d, z, x, mod, w, n_ctx, tr):
    bsz, s, dm = x.shape
    nct = n_ctx // tr
    tile = lambda width: pl.BlockSpec((1, tr, width), lambda bb, j: (bb, j, 0))
    gate = lambda k: pl.BlockSpec((1, tr, dm), lambda bb, j: (bb, j, CB_GATE * LANE // dm + k))
    const = lambda shape: pl.BlockSpec(shape, lambda bb, j: (0,) * len(shape))
    return pl.pallas_call(
        _merge_kernel, name="merge_out",
        grid=(bsz, s // tr),
        in_specs=[tile(A_WIDTH), tile(B_INNER), tile(C_WIDTH), tile(S5_WIDTH),
                  gate(0), gate(1), gate(2), gate(3), tile(dm),
                  pl.BlockSpec((1, 1, 6, dm), lambda bb, j: (bb, jnp.minimum(j // nct, 1), 0, 0)),
                  const((A_WIDTH, dm)), const((B_INNER, dm)), const((C_WIDTH, dm)),
                  const((S5_WIDTH, 2 * dm)), const((1, 2 * dm)), const((dm, dm)), const((1, dm)),
                  const((2, dm, LANE)), const((1, LANE))],
        out_specs=[tile(dm), tile(dm), tile(LANE)],
        out_shape=[jax.ShapeDtypeStruct((bsz, s, dm), F32), jax.ShapeDtypeStruct((bsz, s, dm), BF16),
                   jax.ShapeDtypeStruct((bsz, s, LANE), F32)],
        compiler_params=_cparams(("parallel", "parallel")),
    )(a, b, c, d, z, z, z, z, x, mod, w["proj_a"], w["proj_b"], w["proj_c"], w["glu_w"], w["glu_b"],
      w["w_out"], w["norm_ffn"], w["router_w"], w["router_b"])


def _moe_kernel(be_ref, na_ref, hs_ref, gate_ref, w1g_ref, w1l_ref, b1g_ref, b1l_ref, w2_ref, b2_ref, o_ref):
    i = pl.program_id(0)

    @pl.when(i < na_ref[0])
    def _():
        hs = hs_ref[...]
        glu = jnp.minimum(_dot(hs, w1g_ref[0]) + b1g_ref[0], SWIGLU_LIMIT)
        lin = jnp.clip(_dot(hs, w1l_ref[0]) + b1l_ref[0], -SWIGLU_LIMIT, SWIGLU_LIMIT)
        act = glu * jax.nn.sigmoid(SWIGLU_ALPHA * glu) * (lin + 1.0)
        y = _dot(act.astype(BF16), w2_ref[0]) + b2_ref[0]
        o_ref[...] = (y * gate_ref[...]).astype(o_ref.dtype)

    @pl.when(i >= na_ref[0])
    def _():
        o_ref[...] = jnp.zeros_like(o_ref)


def _moe_ffn(hs, gate_rows, block_exp, n_active, w):
    n_rows, dm = hs.shape
    tm = MOE_TM
    n_blocks = n_rows // tm
    de = w["w1g"].shape[-1]
    wspec = lambda shape: pl.BlockSpec((1,) + shape, lambda i, be, na: (be[i], 0, 0))
    grid_spec = pltpu.PrefetchScalarGridSpec(
        num_scalar_prefetch=2,
        grid=(n_blocks,),
        in_specs=[pl.BlockSpec((tm, dm), lambda i, be, na: (i, 0)),
                  pl.BlockSpec((tm, 1), lambda i, be, na: (i, 0)),
                  wspec((dm, de)), wspec((dm, de)), wspec((1, de)), wspec((1, de)),
                  wspec((de, dm)), wspec((1, dm))],
        out_specs=pl.BlockSpec((tm, dm), lambda i, be, na: (i, 0)),
    )
    return pl.pallas_call(
        _moe_kernel, grid_spec=grid_spec, name="moe_ffn",
        out_shape=jax.ShapeDtypeStruct((n_rows, dm), BF16),
        compiler_params=_cparams(("arbitrary",)),
    )(block_exp, n_active, hs, gate_rows, w["w1g"], w["w1l"], w["b1g"], w["b1l"], w["w2"], w["b2"])


def _w1_split_kernel(w_ref, perm_ref, g_ref, l_ref):
    w = w_ref[0, 0].astype(BF16)
    for c in range(w.shape[1] // (2 * LANE)):
        r = _dot(w[:, c * 2 * LANE:(c + 1) * 2 * LANE], perm_ref[...])
        g_ref[0, :, c * LANE:(c + 1) * LANE] = r[:, :LANE].astype(BF16)
        l_ref[0, :, c * LANE:(c + 1) * LANE] = r[:, LANE:].astype(BF16)


def _w1_split(w1_all, layer):
    _, n_exp, dm, de2 = w1_all.shape
    tk = 512
    src = np.arange(2 * LANE)[:, None]
    dst = np.arange(2 * LANE)[None, :]
    perm = jnp.asarray(np.where(dst < LANE, src == 2 * dst, src == 2 * (dst - LANE) + 1), BF16)
    out = jax.ShapeDtypeStruct((n_exp, dm, de2 // 2), BF16)
    ospec = pl.BlockSpec((1, tk, de2 // 2), lambda e, i: (e, i, 0))
    return pl.pallas_call(
        _w1_split_kernel, name="w1_split",
        grid=(n_exp, dm // tk),
        in_specs=[pl.BlockSpec((1, 1, tk, de2), lambda e, i: (layer, e, i, 0)),
                  pl.BlockSpec((2 * LANE, 2 * LANE), lambda e, i: (0, 0))],
        out_specs=[ospec, ospec], out_shape=[out, out],
        compiler_params=_cparams(("parallel", "parallel")),
    )(w1_all, perm)


def _route(logits, n_tok):
    tm = MOE_TM
    top_val, top_idx = lax.top_k(logits, TOP_K)
    gate = jax.nn.softmax(top_val, axis=-1).reshape(-1)
    n_assign = n_tok * TOP_K
    e_flat = top_idx.reshape(-1).astype(jnp.int32)
    ar = jnp.arange(n_assign, dtype=jnp.int32)
    e_sorted, order = lax.sort((e_flat, ar), num_keys=1, is_stable=True)
    _, inv = lax.sort((order, ar), num_keys=1)
    counts = jnp.sum(jax.nn.one_hot(e_flat, N_EXPERTS, dtype=jnp.int32), axis=0)
    padded = (counts + tm - 1) // tm * tm
    start = jnp.cumsum(counts) - counts
    pend = jnp.cumsum(padded)
    pstart = pend - padded
    n_rows = -(-n_assign // tm) * tm + N_EXPERTS * tm
    n_blocks = n_rows // tm
    block_exp = jnp.minimum(jnp.searchsorted(pend, jnp.arange(n_blocks, dtype=jnp.int32) * tm, side="right"),
                            N_EXPERTS - 1).astype(jnp.int32)
    n_active = (pend[-1] // tm).astype(jnp.int32).reshape(1)
    row = jnp.arange(n_rows, dtype=jnp.int32).reshape(n_blocks, tm)
    within = row - pstart[block_exp][:, None]
    valid = jnp.logical_and(within < counts[block_exp][:, None], row < pend[-1]).reshape(-1)
    idx = jnp.clip(start[block_exp][:, None] + within, 0, n_assign - 1).reshape(-1)
    src = jnp.take(order, idx, mode="clip")
    buf_tok = jnp.where(valid, src // TOP_K, 0)
    buf_gate = jnp.where(valid, jnp.take(gate, src, mode="clip"), 0.0)
    dest = (pstart[e_sorted] + jnp.arange(n_assign, dtype=jnp.int32) - start[e_sorted])[inv]
    return buf_tok, buf_gate.reshape(n_rows, 1), block_exp, n_active, dest.reshape(n_tok, TOP_K)


def _s5_tables(lam_re, lam_im, log_step, b_re, b_im, c_re, c_im, d_skip):
    lc = S5_CHUNK
    lam = lax.complex(lam_re.astype(F32), lam_im.astype(F32))
    step = jnp.exp(log_step.astype(F32))[..., None]
    lam_dt = lam * step
    lam_bar = jnp.exp(lam_dt)
    b_cplx = lax.complex(b_re.astype(F32), b_im.astype(F32))
    c_cplx = lax.complex(c_re.astype(F32), c_im.astype(F32))
    b_bar = ((lam_bar - 1.0) / lam)[..., None] * b_cplx[None]
    kk = jnp.arange(lc + 1, dtype=F32)
    pw = jnp.exp(lam_dt[..., None] * kk)
    kern = jnp.einsum("gcp,dgpk,dgpe->dgkce", c_cplx, pw[..., :lc], b_bar).real
    t_idx = jnp.arange(lc)
    lag = t_idx[:, None] - t_idx[None, :]
    kf = kern[0][:, jnp.clip(lag, 0, lc - 1)]
    kb = kern[1][:, jnp.clip(-lag, 0, lc - 1)]
    sel_f = (lag >= 0)[None, :, :, None, None]
    sel_b = (lag <= 0)[None, :, :, None, None]
    toep = jnp.where(sel_f, kf, 0.0) + jnp.where(sel_b, kb, 0.0)
    tmat = toep.transpose(0, 2, 4, 1, 3).reshape(S5_GROUPS, lc * S5_GROUP, lc * S5_GROUP)
    qf = jnp.einsum("gps,gpe->gsep", pw[0][..., :lc][..., ::-1], b_bar[0]).reshape(S5_GROUPS, lc * S5_GROUP, S5_STATE)
    qb = jnp.einsum("gps,gpe->gsep", pw[1][..., :lc], b_bar[1]).reshape(S5_GROUPS, lc * S5_GROUP, S5_STATE)
    pf = jnp.einsum("gcp,gpt->gptc", c_cplx, pw[0][..., 1:]).reshape(S5_GROUPS, S5_STATE, lc * S5_GROUP)
    pb = jnp.einsum("gcp,gpt->gptc", c_cplx, pw[1][..., 1:][..., ::-1]).reshape(S5_GROUPS, S5_STATE, lc * S5_GROUP)
    lam_chunk = pw[..., lc]
    npair = S5_GROUPS // 2
    gw = lc * S5_GROUP
    pair = lambda a: a.reshape((npair, 2) + a.shape[1:])
    tm2, qf2, qb2, pf2, pb2 = pair(tmat), pair(qf), pair(qb), pair(pf), pair(pb)
    t_pair = jnp.zeros((npair, 2 * gw, 2 * gw), F32)
    q_pair = jnp.zeros((npair, 2 * gw, 4 * LANE), F32)
    p_pair = jnp.zeros((npair, 4 * LANE, 2 * gw), F32)
    for k in range(2):
        rows = slice(k * gw, (k + 1) * gw)
        t_pair = t_pair.at[:, rows, rows].set(tm2[:, k])
        for d, (qq, pp) in enumerate(((qf2, pf2), (qb2, pb2))):
            c_re_ = slice(2 * d * LANE + k * S5_STATE, 2 * d * LANE + (k + 1) * S5_STATE)
            c_im_ = slice((2 * d + 1) * LANE + k * S5_STATE, (2 * d + 1) * LANE + (k + 1) * S5_STATE)
            q_pair = q_pair.at[:, rows, c_re_].set(qq[:, k].real)
            q_pair = q_pair.at[:, rows, c_im_].set(qq[:, k].imag)
            p_pair = p_pair.at[:, c_re_, rows].set(pp[:, k].real)
            p_pair = p_pair.at[:, c_im_, rows].set(-pp[:, k].imag)
    lam_c = lam_chunk.reshape(2, npair, 2 * S5_STATE)
    lam_rows = jnp.stack([lam_c[0].real, lam_c[0].imag, lam_c[1].real, lam_c[1].imag], axis=1)
    lam_rows = jnp.concatenate([lam_rows, jnp.zeros((npair, 4, LANE), F32)], axis=1)
    dsk = jnp.tile(d_skip.astype(F32).reshape(S5_GROUPS, 1, S5_GROUP), (1, lc, 1)).reshape(npair, 1, 2 * gw)
    s_i, k_i, c_i = np.meshgrid(np.arange(lc), np.arange(2), np.arange(S5_GROUP), indexing="ij")
    idx = (k_i * gw + s_i * S5_GROUP + c_i).reshape(-1)
    t_pair = t_pair[:, idx][:, :, idx]
    q_pair = q_pair[:, idx]
    p_pair = p_pair[:, :, idx]
    dsk = dsk[:, :, idx]
    lanes = np.arange(LANE)
    place = np.zeros((4, 4, LANE, LANE), np.float32)
    for a in range(4):
        for i in range(4):
            place[a, i] = (lanes[:, None] // 32 == a) & (lanes[None, :] // 32 == i) & (lanes[:, None] % 32 == lanes[None, :] % 32)
    return {"t": t_pair.astype(BF16), "q": q_pair.astype(BF16), "p": p_pair.astype(BF16), "lam": lam_rows, "dsk": dsk,
            "place": jnp.asarray(place, BF16)}


def _na_bias_table(rpb):
    cols = jnp.arange(GRID_W)
    col_start = jnp.clip(cols - NA_KW // 2, 0, GRID_W - NA_KW)
    in_win = (cols[None, :] >= col_start[:, None]) & (cols[None, :] < col_start[:, None] + NA_KW)
    dc = jnp.clip(cols[None, :] - cols[:, None], -(NA_KW - 1), NA_KW - 1) + NA_KW - 1
    e = jnp.arange(2 * NA_KH + 2)
    tabs = []
    for i in range(2):
        dr = e - 1 - i
        ok = (dr >= 0) & (dr <= 2 * NA_KH - 2)
        b = rpb.astype(F32)[:, jnp.clip(dr, 0, 2 * NA_KH - 2)][:, :, dc]
        b = jnp.where(ok[None, :, None, None] & in_win[None, None], b, NEG_BIG)
        tabs.append(b.transpose(0, 1, 3, 2))
    return jnp.concatenate(tabs, axis=-1)


def _rope_tables(n_ctx, t_len):
    pos = jnp.arange(t_len)
    row, col = pos // GRID_W, pos % GRID_W
    half = B_STATE // 2
    inv_freq = ROPE_THETA ** (-jnp.arange(0, half, 2, dtype=F32) / half)
    ang_r = row.astype(F32)[:, None] * inv_freq[None, :]
    ang_c = col.astype(F32)[:, None] * inv_freq[None, :]
    cos = jnp.concatenate([jnp.cos(ang_r)] * 2 + [jnp.cos(ang_c)] * 2, axis=-1)
    sin = jnp.concatenate([-jnp.sin(ang_r), jnp.sin(ang_r), -jnp.sin(ang_c), jnp.sin(ang_c)], axis=-1)
    cos = jnp.concatenate([jnp.ones((n_ctx, B_STATE), F32), cos], axis=0)
    sin = jnp.concatenate([jnp.zeros((n_ctx, B_STATE), F32), sin], axis=0)
    return cos, sin


def _ssd_params(conv_w, conv_b, a_log, dt_bias, d_skip, norm_g, cos, sin):
    a = -jnp.exp(a_log.astype(F32)).reshape(-1)
    pad = lambda v: jnp.concatenate([v, jnp.zeros((LANE - v.shape[0],), F32)])
    vec = jnp.zeros((8, LANE), F32).at[0].set(pad(dt_bias.astype(F32).reshape(-1))).at[1].set(pad(a))
    lanes = jnp.arange(LANE)[:, None]
    cols = jnp.arange(B_INNER)[None, :]
    expand = jnp.stack([(lanes == d * B_HEADS + cols // B_HEADDIM) for d in range(2)]).astype(BF16)
    sel = jnp.stack([jnp.broadcast_to(jnp.arange(LANE)[:, None] == j, (LANE, LANE)) for j in range(2 * B_HEADS)]).astype(BF16)
    cw = jnp.concatenate([conv_w.astype(F32), jnp.zeros((8 - B_CONV, B_CONV_DIM), F32)], axis=0)
    return {"conv_w": cw, "conv_b": conv_b.astype(F32).reshape(1, -1), "cos": cos, "sin": sin, "vec": vec,
            "expand": expand, "sel": sel, "dskip": jnp.repeat(d_skip.astype(F32), B_HEADDIM).reshape(1, -1),
            "norm": norm_g.astype(F32).reshape(1, -1)}


def _gla_params(lb, norm_g):
    lbh = lb.astype(F32).reshape(A_HEADS, A_DK)
    rows = jnp.stack([lbh, jnp.log(lbh), jnp.log1p(-lbh), 1.0 - lbh,
                      jnp.broadcast_to(norm_g.astype(F32), (A_HEADS, A_DK))], axis=1)
    return jnp.concatenate([rows, jnp.zeros((A_HEADS, 3, A_DK), F32)], axis=1)


def kernel(x, c, ctx, c_ctx, ada_w, ada_b, norm_mix, norm_ffn, w_in, hgrn_lb_logits, hgrn_norm, mamba_conv_w, mamba_conv_b, mamba_a_log, mamba_dt_bias, mamba_d, mamba_norm, na_rpb, s5_lam_re, s5_lam_im, s5_log_step, s5_b_re, s5_b_im, s5_c_re, s5_c_im, s5_d, s5_glu_w, s5_glu_b, proj_a, proj_b, proj_c, w_out, router_w, router_b, exp_w1, exp_b1, exp_w2, exp_b2, final_norm):
    bsz, t_len, dm = x.shape
    n_ctx = ctx.shape[1]
    s = n_ctx + t_len
    depth = w_in.shape[0]
    tr = math.gcd(n_ctx, 256)
    n_tok = bsz * s
    hp = lax.Precision.HIGHEST

    lb_all = jnp.cumsum(jax.nn.softmax(hgrn_lb_logits.astype(F32), axis=0), axis=0)
    lb_all = lb_all - lb_all[0:1]
    cond = jnp.concatenate([jax.nn.silu(c), jax.nn.silu(c_ctx)[None]], axis=0)
    cos, sin = _rope_tables(n_ctx, t_len)
    d_in = w_in.shape[-1]
    dt_lo = 4096

    xa = jnp.concatenate([ctx, x], axis=1)
    ymoe = None
    mod_prev = None
    for l in range(depth):
        m = (jnp.dot(cond, ada_w[l], precision=hp) + ada_b[l]).reshape(bsz + 1, 6, dm)
        mod = jnp.stack([jnp.broadcast_to(m[bsz], (bsz, 6, dm)), m[:bsz]], axis=1)
        w_l = w_in[l]
        w_re = jnp.concatenate([w_l[:, :dt_lo], w_l[:, dt_lo + 2 * B_HEADS:], w_l[:, dt_lo:dt_lo + 2 * B_HEADS],
                                jnp.zeros((dm, Z_WIDTH - d_in), w_l.dtype)], axis=1).astype(BF16)
        if ymoe is None:
            (h,) = _prep(xa, mod, norm_mix[l], n_ctx, tr)
        else:
            xa, h = _prep(xa, mod, norm_mix[l], n_ctx, tr, ymoe=ymoe, mod_prev=mod_prev)
        z = _in_proj(h.reshape(n_tok, dm), w_re).reshape(bsz, s, Z_WIDTH)

        a_out = _gla(z, _gla_params(lb_all[l], hgrn_norm[l]), n_ctx)
        b_out = _ssd(z, _ssd_params(mamba_conv_w[l], mamba_conv_b[l], mamba_a_log[l], mamba_dt_bias[l],
                                    mamba_d[l], mamba_norm[l], cos, sin), n_ctx)
        c_out = _na(z, _na_bias_table(na_rpb[l]), n_ctx)
        d_out = _s5(z, _s5_tables(s5_lam_re[l], s5_lam_im[l], s5_log_step[l], s5_b_re[l], s5_b_im[l],
                                  s5_c_re[l], s5_c_im[l], s5_d[l]), n_ctx)

        rw = jnp.concatenate([router_w[l].astype(F32), jnp.zeros((dm, LANE - N_EXPERTS), F32)], axis=1)
        rw_hi = rw.astype(BF16)
        rw_lo = (rw - rw_hi.astype(F32)).astype(BF16)
        wts = {"proj_a": proj_a[l].astype(BF16), "proj_b": proj_b[l].astype(BF16), "proj_c": proj_c[l].astype(BF16),
               "glu_w": s5_glu_w[l].astype(BF16), "glu_b": s5_glu_b[l].astype(F32).reshape(1, -1),
               "w_out": w_out[l].astype(BF16), "norm_ffn": norm_ffn[l].astype(F32).reshape(1, -1),
               "router_w": jnp.stack([rw_hi, rw_lo]),
               "router_b": jnp.concatenate([router_b[l].astype(F32), jnp.zeros((LANE - N_EXPERTS,), F32)]).reshape(1, -1)}
        xa, h2, logits = _merge(a_out, b_out, c_out, d_out, z, xa, mod, wts, n_ctx, tr)

        buf_tok, buf_gate, block_exp, n_active, dest = _route(logits.reshape(n_tok, LANE)[:, :N_EXPERTS], n_tok)
        hs = jnp.take(h2.reshape(n_tok, dm), buf_tok, axis=0, mode="clip")
        w1g, w1l = _w1_split(exp_w1, l)
        ew = {"w1g": w1g, "w1l": w1l,
              "b1g": exp_b1[l][:, None, 0::2].astype(F32), "b1l": exp_b1[l][:, None, 1::2].astype(F32),
              "w2": exp_w2[l].astype(BF16), "b2": exp_b2[l][:, None, :].astype(F32)}
        yb = _moe_ffn(hs, buf_gate, block_exp, n_active, ew)
        ymoe = jnp.take(yb, dest.T, axis=0, mode="clip").reshape(TOP_K, bsz, s, dm)
        mod_prev = mod

    zero_mod = jnp.zeros_like(mod_prev)
    (out,) = _prep(xa, zero_mod, final_norm, n_ctx, tr, ymoe=ymoe, mod_prev=mod_prev, final=True)
    return out
```

```python
import functools
import math

import numpy as np
import jax
import jax.numpy as jnp
from jax import lax
from jax.experimental import pallas as pl
from jax.experimental.pallas import tpu as pltpu

F32 = jnp.float32
BF16 = jnp.bfloat16

D_MODEL = 1024
GRID_W = 64
EPS = 1e-6
N_BRANCH = 4
A_HEADS = 4
A_DK = 128
A_WIDTH = 512
B_HEADS = 8
B_HEADDIM = 64
B_INNER = 512
B_GROUPS = 2
B_STATE = 128
B_CONV = 5
B_CONV_DIM = 1024
ROPE_THETA = 10000.0
C_HEADS = 8
C_HEADDIM = 64
C_WIDTH = 512
NA_KH = 8
NA_KW = 16
S5_GROUP = 16
S5_GROUPS = 32
S5_WIDTH = 512
S5_STATE = 64
N_EXPERTS = 32
TOP_K = 4
D_EXPERT = 1024
SWIGLU_LIMIT = 7.0
SWIGLU_ALPHA = 1.702

LANE = 128
GLA_BLK = 16
CHUNK = 128
S5_CHUNK = 16
NA_WIN = 10
MOE_TM = 512
NEG_BIG = -1e30
VMEM_LIMIT = 56 * 1024 * 1024

CB_Q, CB_FF, CB_FB, CB_I, CB_G = 0, 4, 8, 12, 16
CB_MZ, CB_XBC, CB_QKV, CB_S5, CB_GATE, CB_DT = 20, 24, 32, 44, 48, 80
Z_WIDTH = 81 * LANE
Z_TN = 9 * LANE


def _dot(a, b):
    return jnp.dot(a, b, preferred_element_type=F32)


def _dot_nt(a, b):
    return lax.dot_general(a, b, (((1,), (1,)), ((), ())), preferred_element_type=F32)


def _hi_lo(x):
    hi = x.astype(BF16)
    lo = (x - hi.astype(F32)).astype(BF16)
    return hi, lo


def _sel_dot(mat, x):
    hi, lo = _hi_lo(x)
    return _dot(mat, hi) + _dot(mat, lo)


def _dot_sel(x, mat):
    hi, lo = _hi_lo(x)
    return _dot(hi, mat) + _dot(lo, mat)


def _cparams(sem):
    return pltpu.CompilerParams(dimension_semantics=sem, vmem_limit_bytes=VMEM_LIMIT)


def _prep_kernel(*refs, has_moe, final):
    if has_moe:
        x_ref, y_ref, modp_ref, mod_ref, g_ref = refs[:5]
        outs = refs[5:]
    else:
        x_ref, mod_ref, g_ref = refs[:3]
        outs = refs[3:]
    x = x_ref[0]
    if has_moe:
        f = (y_ref[0, 0].astype(F32) + y_ref[1, 0].astype(F32)) + (y_ref[2, 0].astype(F32) + y_ref[3, 0].astype(F32))
        x = x + modp_ref[0, 0, 5:6, :] * f
    y = x * lax.rsqrt(jnp.mean(x * x, axis=-1, keepdims=True) + EPS) * g_ref[...]
    if final:
        outs[0][0] = y
        return
    h = y * (1.0 + mod_ref[0, 0, 1:2, :]) + mod_ref[0, 0, 0:1, :]
    if has_moe:
        outs[0][0] = x
        outs[1][0] = h.astype(BF16)
    else:
        outs[0][0] = h.astype(BF16)


def _prep(x, mod, g, n_ctx, tr, ymoe=None, mod_prev=None, final=False):
    bsz, s, d = x.shape
    nct = n_ctx // tr
    has_moe = ymoe is not None
    kind = lambda j: jnp.minimum(j // nct, 1)
    if final:
        grid = (bsz, (s - n_ctx) // tr)
        off = nct
    else:
        grid = (bsz, s // tr)
        off = 0
    x_spec = pl.BlockSpec((1, tr, d), lambda b, j: (b, j + off, 0))
    mod_spec = pl.BlockSpec((1, 1, 6, d), lambda b, j: (b, kind(j + off), 0, 0))
    g_spec = pl.BlockSpec((1, d), lambda b, j: (0, 0))
    in_specs, args = [x_spec], [x]
    if has_moe:
        in_specs += [pl.BlockSpec((4, 1, tr, d), lambda b, j: (0, b, j + off, 0)), mod_spec]
        args += [ymoe, mod_prev]
    in_specs += [mod_spec, g_spec]
    args += [mod, g.reshape(1, d)]
    if final:
        out_shape = [jax.ShapeDtypeStruct((bsz, s - n_ctx, d), F32)]
        out_specs = [pl.BlockSpec((1, tr, d), lambda b, j: (b, j, 0))]
    elif has_moe:
        out_shape = [jax.ShapeDtypeStruct((bsz, s, d), F32), jax.ShapeDtypeStruct((bsz, s, d), BF16)]
        out_specs = [x_spec, x_spec]
    else:
        out_shape = [jax.ShapeDtypeStruct((bsz, s, d), BF16)]
        out_specs = [x_spec]
    return pl.pallas_call(
        functools.partial(_prep_kernel, has_moe=has_moe, final=final), name="prep_norm",
        grid=grid, in_specs=in_specs, out_specs=out_specs, out_shape=out_shape,
        compiler_params=_cparams(("parallel", "parallel")),
    )(*args)


def _matmul_kernel(h_ref, w_ref, z_ref):
    z_ref[...] = _dot(h_ref[...], w_ref[...]).astype(z_ref.dtype)


def _in_proj(h2d, w):
    m, k = h2d.shape
    n = w.shape[1]
    tm = next(t for t in (1024, 512, 256, 128) if m % t == 0)
    return pl.pallas_call(
        _matmul_kernel, name="in_proj",
        grid=(n // Z_TN, m // tm),
        in_specs=[pl.BlockSpec((tm, k), lambda j, i: (i, 0)), pl.BlockSpec((k, Z_TN), lambda j, i: (0, j))],
        out_specs=pl.BlockSpec((tm, Z_TN), lambda j, i: (i, j)),
        out_shape=jax.ShapeDtypeStruct((m, n), BF16),
        compiler_params=_cparams(("parallel", "parallel")),
    )(h2d, w)


def _gla_kernel(q_ref, ff_ref, fb_ref, i_ref, g_ref, p_ref, o_ref,
                b_scr, k_scr, v_scr, qt_scr, dec_scr, od_scr, x_scr, *, n_ctx, s_len):
    nchunk = s_len // CHUNK
    nblk = s_len // GLA_BLK
    nb_ctx = n_ctx // GLA_BLK
    bpc = CHUNK // GLA_BLK
    scale = A_DK ** -0.5
    log_lb = p_ref[0, 1:2, :]
    log_1mlb = p_ref[0, 2:3, :]
    omlb = p_ref[0, 3:4, :]
    ng = p_ref[0, 4:5, :]
    ri = lax.broadcasted_iota(jnp.int32, (CHUNK, CHUNK), 0)
    ci = lax.broadcasted_iota(jnp.int32, (CHUNK, CHUNK), 1)
    same = (ri // GLA_BLK) == (ci // GLA_BLK)
    tri = (jnp.where(same, jnp.where(ci <= ri, 1.0, 0.0), 0.0).astype(BF16),
           jnp.where(same, jnp.where(ci >= ri, 1.0, 0.0), 0.0).astype(BF16))
    blk_sum = jnp.where(same, 1.0, 0.0).astype(BF16)
    cblk = ci // GLA_BLK

    def bulk(c, _):
        r0 = pl.multiple_of(c * CHUNK, CHUNK)
        q = q_ref[0, pl.ds(r0, CHUNK), :].astype(F32) * scale
        v = i_ref[0, pl.ds(r0, CHUNK), :].astype(F32)
        v_scr[pl.ds(r0, CHUNK), :] = v
        v_t = v.T
        for d, fz_ref in enumerate((ff_ref, fb_ref)):
            fz = fz_ref[0, pl.ds(r0, CHUNK), :].astype(F32)
            e = jnp.exp(-jnp.abs(fz))
            log_sig = jnp.minimum(fz, 0.0) - jnp.log1p(e)
            k = omlb * (jnp.where(fz >= 0, e, 1.0) / (1.0 + e))
            t2 = log_1mlb + log_sig
            lf = jnp.maximum(log_lb, t2) + jnp.log1p(jnp.exp(-jnp.abs(log_lb - t2)))
            hi, lo = _hi_lo(lf)
            bcum = _dot(tri[d], hi) + _dot(tri[d], lo)
            tot = _dot(blk_sum, hi) + _dot(blk_sum, lo)
            b_scr[d, pl.ds(r0, CHUNK), :] = bcum
            k_scr[d, pl.ds(r0, CHUNK), :] = k
            qt_scr[d, pl.ds(r0, CHUNK), :] = (q * jnp.exp(bcum)).astype(BF16)
            dec_scr[d, pl.ds(r0, CHUNK), :] = jnp.exp(tot)
            kt = (k * jnp.exp(tot - bcum)).astype(BF16)
            for j in range(bpc):
                x_scr[d, c * bpc + j] = _dot(jnp.where(cblk == j, v_t, 0.0).astype(BF16), kt).astype(BF16)
        return 0

    lax.fori_loop(0, nchunk, bulk, 0, unroll=2)

    half = GLA_BLK // 2
    tio = lax.broadcasted_iota(jnp.int32, (half, LANE), 0)
    ones_red = jnp.ones((LANE, LANE), BF16)

    def diag(i, _):
        r0 = pl.multiple_of(i * GLA_BLK, GLA_BLK)
        qq = q_ref[0, pl.ds(r0, GLA_BLK), :].astype(F32) * scale
        q_h = (qq[:half], qq[half:])
        for d in range(2):
            bb = b_scr[d, pl.ds(r0, GLA_BLK), :]
            b_h = (bb[:half], bb[half:])
            pieces, plan = [], []
            for s in range(GLA_BLK):
                bs = b_scr[d, pl.ds(r0 + s, 1), :]
                ks = k_scr[d, pl.ds(r0 + s, 1), :]
                for hf in range(2):
                    lo_t, hi_t = hf * half, hf * half + half - 1
                    if (d == 0 and hi_t < s) or (d == 1 and lo_t > s):
                        continue
                    diff = b_h[hf] - bs
                    if (d == 0 and lo_t < s) or (d == 1 and hi_t > s):
                        keep = (tio + lo_t >= s) if d == 0 else (tio + lo_t <= s)
                        diff = jnp.where(keep, diff, -jnp.inf)
                    pieces.append((q_h[hf] * jnp.exp(diff)) * ks)
                    plan.append((s, hf))
            red = _dot(jnp.concatenate(pieces, axis=0).astype(BF16), ones_red)
            od = [jnp.zeros((half, LANE), F32), jnp.zeros((half, LANE), F32)]
            for n, (s, hf) in enumerate(plan):
                od[hf] = od[hf] + red[n * half:(n + 1) * half, :] * v_scr[pl.ds(r0 + s, 1), :]
            od_scr[d, pl.ds(r0, GLA_BLK), :] = jnp.concatenate(od, axis=0)
        return 0

    lax.fori_loop(0, nblk, diag, 0, unroll=4)

    def scan(i, carry):
        new = []
        order = (i, jnp.where(i < nb_ctx, nb_ctx - 1 - i, nblk - 1 - (i - nb_ctx)))
        for d in range(2):
            j = order[d]
            st = carry[d]
            ds_blk = x_scr[d, j].astype(F32)
            x_scr[d, j] = st.astype(BF16)
            new.append(st * dec_scr[d, pl.ds(j * GLA_BLK, 1), :] + ds_blk)
        return tuple(new)

    zero_state = jnp.zeros((LANE, LANE), F32)
    lax.fori_loop(0, nblk, scan, (zero_state, zero_state))

    def emit(c, _):
        r0 = pl.multiple_of(c * CHUNK, CHUNK)
        parts = []
        for j in range(bpc):
            rows = pl.ds(r0 + j * GLA_BLK, GLA_BLK)
            o = od_scr[0, rows, :] + od_scr[1, rows, :]
            o = o + _dot_nt(qt_scr[0, rows, :], x_scr[0, c * bpc + j])
            parts.append(o + _dot_nt(qt_scr[1, rows, :], x_scr[1, c * bpc + j]))
        o = jnp.concatenate(parts, axis=0)
        o = o * lax.rsqrt(jnp.mean(o * o, axis=-1, keepdims=True) + EPS) * ng
        g = g_ref[0, pl.ds(r0, CHUNK), :].astype(F32)
        o_ref[0, pl.ds(r0, CHUNK), :] = (o * (g * jax.nn.sigmoid(g))).astype(BF16)
        return 0

    lax.fori_loop(0, nchunk, emit, 0, unroll=2)


def _gla(z, params, n_ctx):
    bsz, s, _ = z.shape
    spec = lambda cb: pl.BlockSpec((1, s, LANE), lambda b, h: (b, 0, cb + h))
    return pl.pallas_call(
        functools.partial(_gla_kernel, n_ctx=n_ctx, s_len=s), name="gla_mixer",
        grid=(bsz, A_HEADS),
        in_specs=[spec(CB_Q), spec(CB_FF), spec(CB_FB), spec(CB_I), spec(CB_G),
                  pl.BlockSpec((1, 8, LANE), lambda b, h: (h, 0, 0))],
        out_specs=pl.BlockSpec((1, s, LANE), lambda b, h: (b, 0, h)),
        out_shape=jax.ShapeDtypeStruct((bsz, s, A_WIDTH), BF16),
        scratch_shapes=[
            pltpu.VMEM((2, s, LANE), F32), pltpu.VMEM((2, s, LANE), F32), pltpu.VMEM((s, LANE), F32),
            pltpu.VMEM((2, s, LANE), BF16), pltpu.VMEM((2, s, LANE), F32), pltpu.VMEM((2, s, LANE), F32),
            pltpu.VMEM((2, s // GLA_BLK, LANE, LANE), BF16),
        ],
        compiler_params=_cparams(("parallel", "parallel")),
    )(z, z, z, z, z, params)


def _ssd_kernel(mz_ref, xbc_ref, dt_ref, cw_ref, cb_ref, cos_ref, sin_ref, vec_ref, exp_ref, sel_ref,
                dsk_ref, ng_ref, o_ref, xs_scr, bm_scr, cm_scr, dt_scr, y_scr, *, n_ctx, s_len):
    nchunk = s_len // CHUNK
    nc_ctx = n_ctx // CHUNK
    halo = 16
    ri = lax.broadcasted_iota(jnp.int32, (CHUNK, CHUNK), 0)
    ci = lax.broadcasted_iota(jnp.int32, (CHUNK, CHUNK), 1)
    lane = lax.broadcasted_iota(jnp.int32, (1, LANE), 1)
    dt_bias = vec_ref[0:1, :]
    a_lane = vec_ref[1:2, :]
    dt_valid = lane < 2 * B_HEADS

    wi_r = lax.broadcasted_iota(jnp.int32, (CHUNK, CHUNK + 2 * halo), 0)
    wi_c = lax.broadcasted_iota(jnp.int32, (CHUNK, CHUNK + 2 * halo), 1)
    shift = [jnp.where(wi_c == wi_r + halo + o, 1.0, 0.0).astype(BF16) for o in range(-(B_CONV // 2), B_CONV // 2 + 1)]

    def prep(c, _):
        r0 = pl.multiple_of(c * CHUNK, CHUNK)
        same_prev = jnp.logical_and(c > 0, c != nc_ctx)
        same_next = jnp.logical_and(c < nchunk - 1, c != nc_ctx - 1)
        rp = pl.multiple_of(jnp.maximum(r0 - halo, 0), halo)
        rn = pl.multiple_of(jnp.minimum(r0 + CHUNK, s_len - halo), halo)
        prev = jnp.where(same_prev, xbc_ref[0, pl.ds(rp, halo), :].astype(F32), 0.0).astype(BF16)
        nxt = jnp.where(same_next, xbc_ref[0, pl.ds(rn, halo), :].astype(F32), 0.0).astype(BF16)
        win = jnp.concatenate([prev, xbc_ref[0, pl.ds(r0, CHUNK), :], nxt], axis=0)
        u = jnp.zeros((CHUNK, B_CONV_DIM), F32) + cb_ref[...]
        for j in range(B_CONV):
            u = u + _dot(shift[j], win) * cw_ref[j:j + 1, :]
        u = u * jax.nn.sigmoid(u)
        xs_scr[pl.ds(r0, CHUNK), :] = u[:, :B_INNER].astype(BF16)
        cos = cos_ref[pl.ds(r0, CHUNK), :]
        sin = sin_ref[pl.ds(r0, CHUNK), :]
        first = (lane % 64) < 32
        for k, scr in ((0, bm_scr), (1, cm_scr)):
            for g in range(B_GROUPS):
                lo = B_INNER + (k * B_GROUPS + g) * B_STATE
                vv = u[:, lo:lo + B_STATE]
                partner = jnp.where(first, pltpu.roll(vv, LANE - 32, axis=1), pltpu.roll(vv, 32, axis=1))
                scr[pl.ds(r0, CHUNK), g * B_STATE:(g + 1) * B_STATE] = (vv * cos + partner * sin).astype(BF16)
        xd = dt_ref[0, pl.ds(r0, CHUNK), :].astype(F32) + dt_bias
        dt = jnp.maximum(xd, 0.0) + jnp.log1p(jnp.exp(-jnp.abs(xd)))
        dt_scr[pl.ds(r0, CHUNK), :] = jnp.where(dt_valid, dt, 0.0)
        y_scr[pl.ds(r0, CHUNK), :] = u[:, :B_INNER] * dsk_ref[...]
        return 0

    lax.fori_loop(0, nchunk, prep, 0)

    tri = (jnp.where(ci <= ri, 1.0, 0.0).astype(BF16), jnp.where(ci >= ri, 1.0, 0.0).astype(BF16))
    keep = (ci <= ri, ci >= ri)
    lane_lo = lax.broadcasted_iota(jnp.int32, (1, LANE), 1) < B_HEADDIM
    gw = B_INNER // B_GROUPS

    def step(i, carry):
        new = []
        order = (i, jnp.where(i < nc_ctx, nc_ctx - 1 - i, nchunk - 1 - (i - nc_ctx)))
        for d in range(2):
            c = order[d]
            r0 = pl.multiple_of(c * CHUNK, CHUNK)
            st = carry[d]
            dt = dt_scr[pl.ds(r0, CHUNK), :]
            acum = _sel_dot(tri[d], dt * a_lane)
            acum_t = acum.T
            expand = exp_ref[d]
            xdt = xs_scr[pl.ds(r0, CHUNK), :].astype(F32) * _dot_sel(dt, expand)
            ea_x = _dot_sel(jnp.exp(acum), expand)
            last = CHUNK - 1 if d == 0 else 0
            total = acum[last:last + 1, :]
            w_x = _dot_sel(jnp.exp(total - acum), expand)
            dec_row = ea_x[last:last + 1, :]
            xw_b = (xdt * w_x).astype(BF16)
            hi, lo = _hi_lo(acum)
            y_parts = []
            s_parts = []
            for g in range(B_GROUPS):
                bm = bm_scr[pl.ds(r0, CHUNK), g * B_STATE:(g + 1) * B_STATE]
                cm = cm_scr[pl.ds(r0, CHUNK), g * B_STATE:(g + 1) * B_STATE]
                gmat = _dot_nt(cm, bm)
                y_off = _dot(cm, st[:, g * gw:(g + 1) * gw].astype(BF16))
                pair_out = []
                for pp in range(2):
                    p = g * 2 + pp
                    xp = xdt[:, p * LANE:(p + 1) * LANE]
                    acc = jnp.zeros((CHUNK, LANE), F32)
                    for hh in range(2):
                        j = d * B_HEADS + 2 * p + hh
                        col = _dot(hi, sel_ref[j]) + _dot(lo, sel_ref[j])
                        seg = col - acum_t[j:j + 1, :]
                        m = (gmat * jnp.exp(jnp.where(keep[d], seg, -jnp.inf))).astype(BF16)
                        xm = jnp.where(lane_lo, xp, 0.0) if hh == 0 else jnp.where(lane_lo, 0.0, xp)
                        acc = acc + _dot(m, xm.astype(BF16))
                    pair_out.append(acc)
                y_parts.append(jnp.concatenate(pair_out, axis=1) + y_off * ea_x[:, g * gw:(g + 1) * gw])
                bm_t = bm.astype(F32).T.astype(BF16)
                s_parts.append(_dot(bm_t, xw_b[:, g * gw:(g + 1) * gw]))
            y_scr[pl.ds(r0, CHUNK), :] += jnp.concatenate(y_parts, axis=1)
            new.append(st * dec_row + jnp.concatenate(s_parts, axis=1))
        return tuple(new)

    zero_state = jnp.zeros((B_STATE, B_INNER), F32)
    lax.fori_loop(0, nchunk, step, (zero_state, zero_state))

    def finish(c, _):
        r0 = pl.multiple_of(c * CHUNK, CHUNK)
        zz = mz_ref[0, pl.ds(r0, CHUNK), :].astype(F32)
        y = y_scr[pl.ds(r0, CHUNK), :] * (zz * jax.nn.sigmoid(zz))
        y = y * lax.rsqrt(jnp.mean(y * y, axis=-1, keepdims=True) + EPS) * ng_ref[...]
        o_ref[0, pl.ds(r0, CHUNK), :] = y.astype(BF16)
        return 0

    lax.fori_loop(0, nchunk, finish, 0)


def _ssd(z, p, n_ctx):
    bsz, s, _ = z.shape
    const = lambda shape: pl.BlockSpec(shape, lambda b: (0,) * len(shape))
    return pl.pallas_call(
        functools.partial(_ssd_kernel, n_ctx=n_ctx, s_len=s), name="ssd_mixer",
        grid=(bsz,),
        in_specs=[
            pl.BlockSpec((1, s, B_INNER), lambda b: (b, 0, CB_MZ * LANE // B_INNER)),
            pl.BlockSpec((1, s, B_CONV_DIM), lambda b: (b, 0, CB_XBC * LANE // B_CONV_DIM)),
            pl.BlockSpec((1, s, LANE), lambda b: (b, 0, CB_DT)),
            const((8, B_CONV_DIM)), const((1, B_CONV_DIM)), const((s, LANE)), const((s, LANE)),
            const((8, LANE)), const((2, LANE, B_INNER)), const((2 * B_HEADS, LANE, LANE)),
            const((1, B_INNER)), const((1, B_INNER)),
        ],
        out_specs=pl.BlockSpec((1, s, B_INNER), lambda b: (b, 0, 0)),
        out_shape=jax.ShapeDtypeStruct((bsz, s, B_INNER), BF16),
        scratch_shapes=[
            pltpu.VMEM((s, B_INNER), BF16), pltpu.VMEM((s, B_GROUPS * B_STATE), BF16),
            pltpu.VMEM((s, B_GROUPS * B_STATE), BF16), pltpu.VMEM((s, LANE), F32), pltpu.VMEM((s, B_INNER), F32),
        ],
        compiler_params=_cparams(("parallel",)),
    )(z, z, z, p["conv_w"], p["conv_b"], p["cos"], p["sin"], p["vec"], p["expand"], p["sel"], p["dskip"], p["norm"])


def _na_kernel(q_ref, k_ref, v_ref, bias_ref, o_ref, vt_scr, *, n_ctx, s_len):
    rows = (s_len - n_ctx) // GRID_W
    n_pairs = rows // 2
    n_vt = s_len // LANE
    nct = n_ctx // LANE
    win = NA_WIN * GRID_W
    scale = C_HEADDIM ** -0.5
    lane = lax.broadcasted_iota(jnp.int32, (1, LANE), 1)
    head_lanes = (lane < C_HEADDIM, lane >= C_HEADDIM)
    sub_lo = lax.broadcasted_iota(jnp.int32, (LANE, 1), 0) < C_HEADDIM

    def transpose_v(t, _):
        r0 = pl.multiple_of(t * LANE, LANE)
        vt_scr[t] = v_ref[0, pl.ds(r0, LANE), :].astype(F32).T.astype(BF16)
        return 0

    lax.fori_loop(0, n_vt, transpose_v, 0)

    k_ctx = k_ref[0, 0:n_ctx, :]

    qc = q_ref[0, 0:n_ctx, :].astype(F32) * scale
    v_ctx = v_ref[0, 0:n_ctx, :]
    out_c = jnp.zeros((n_ctx, LANE), F32)
    for hh in range(2):
        qm = jnp.where(head_lanes[hh], qc, 0.0).astype(BF16)
        sc = _dot_nt(qm, k_ctx)
        pc = jnp.exp(sc - jnp.max(sc, axis=-1, keepdims=True))
        oc = _dot(pc.astype(BF16), v_ctx) / jnp.sum(pc, axis=-1, keepdims=True)
        out_c = jnp.where(head_lanes[hh], oc, out_c)
    o_ref[0, 0:n_ctx, :] = out_c.astype(BF16)

    def pair(rp, _):
        r0 = 2 * rp
        su = jnp.clip(r0 - NA_KH // 2, 0, rows - NA_WIN)
        q0 = pl.multiple_of(n_ctx + r0 * GRID_W, LANE)
        k0 = pl.multiple_of(n_ctx + su * GRID_W, LANE)
        qp = q_ref[0, pl.ds(q0, LANE), :].astype(F32) * scale
        kw = k_ref[0, pl.ds(k0, win), :]
        vt0 = nct + su // 2
        starts = [jnp.clip(r0 + i - NA_KH // 2, 0, rows - NA_KH) for i in range(2)]
        out_t = []
        for hh in range(2):
            qm = jnp.where(head_lanes[hh], qp, 0.0).astype(BF16)
            st = _dot_nt(kw, qm)
            sc = _dot_nt(k_ctx, qm)
            tiles = []
            for w in range(NA_WIN):
                kr = su + w
                ok = [jnp.logical_and(kr >= starts[i], kr < starts[i] + NA_KH).astype(jnp.int32) for i in range(2)]
                okv = jnp.where(lane < GRID_W, ok[0], ok[1]) > 0
                bias = bias_ref[hh, kr - r0 + NA_KH]
                tiles.append(jnp.where(okv, st[w * GRID_W:(w + 1) * GRID_W, :] + bias, NEG_BIG))
            sw = jnp.concatenate(tiles, axis=0)
            m = jnp.maximum(jnp.max(sw, axis=0, keepdims=True), jnp.max(sc, axis=0, keepdims=True))
            pw = jnp.exp(sw - m)
            pc = jnp.exp(sc - m)
            den = jnp.sum(pw, axis=0, keepdims=True) + jnp.sum(pc, axis=0, keepdims=True)
            pw = pw.astype(BF16)
            pc = pc.astype(BF16)
            acc = jnp.zeros((LANE, LANE), F32)
            for t in range(win // LANE):
                acc = acc + _dot(vt_scr[vt0 + t], pw[t * LANE:(t + 1) * LANE, :])
            for t in range(nct):
                acc = acc + _dot(vt_scr[t], pc[t * LANE:(t + 1) * LANE, :])
            out_t.append(acc / den)
        o_ref[0, pl.ds(q0, LANE), :] = jnp.where(sub_lo, out_t[0], out_t[1]).T.astype(BF16)
        return 0

    lax.fori_loop(0, n_pairs, pair, 0)


def _na(z, bias, n_ctx):
    bsz, s, _ = z.shape
    spec = lambda cb: pl.BlockSpec((1, s, LANE), lambda b, p: (b, 0, cb + p))
    return pl.pallas_call(
        functools.partial(_na_kernel, n_ctx=n_ctx, s_len=s), name="na_mixer",
        grid=(bsz, C_HEADS // 2),
        in_specs=[spec(CB_QKV), spec(CB_QKV + 4), spec(CB_QKV + 8),
                  pl.BlockSpec((2, 2 * NA_KH + 2, GRID_W, LANE), lambda b, p: (p, 0, 0, 0))],
        out_specs=pl.BlockSpec((1, s, LANE), lambda b, p: (b, 0, p)),
        out_shape=jax.ShapeDtypeStruct((bsz, s, C_WIDTH), BF16),
        scratch_shapes=[pltpu.VMEM((s // LANE, LANE, LANE), BF16)],
        compiler_params=_cparams(("parallel", "parallel")),
    )(z, z, z, bias)


def _s5_pack_kernel(u_ref, place_ref, r_ref, x_scr, *, nj):
    x_scr[...] = u_ref[0].astype(F32)
    xs = [x_scr[pl.ds(s, nj, stride=S5_CHUNK), :].astype(BF16) for s in range(S5_CHUNK)]
    for a in range(4):
        for m in range(4):
            acc = jnp.zeros((nj, LANE), F32)
            for i in range(4):
                acc = acc + _dot(xs[4 * m + i], place_ref[a, i])
            r_ref[a, 0, :, m * LANE:(m + 1) * LANE] = acc.astype(BF16)


def _s5_unpack_kernel(y_ref, place_ref, o_ref, o_scr, *, nj):
    for t in range(S5_CHUNK):
        acc = jnp.zeros((nj, LANE), F32)
        for a in range(4):
            acc = acc + _dot(y_ref[a, 0, :, (t // 4) * LANE:(t // 4 + 1) * LANE], place_ref[t % 4, a])
        o_scr[pl.ds(t, nj, stride=S5_CHUNK), :] = acc
    o_ref[0] = o_scr[...].astype(BF16)


def _s5_kernel(u_ref, t_ref, q_ref, p_ref, lam_ref, dsk_ref, o_ref, w_scr, z_scr, *, n_ctx, s_len, bsz):
    nj = s_len // S5_CHUNK
    nj_ctx = n_ctx // S5_CHUNK
    qmat = q_ref[0]

    def project(b, _):
        r0 = pl.multiple_of(b * nj, 8)
        w = _dot(u_ref[0, b], qmat)
        for k in range(4):
            w_scr[k, pl.ds(r0, nj), :] = w[:, k * LANE:(k + 1) * LANE]
        return 0

    lax.fori_loop(0, bsz, project, 0)

    lam = [lam_ref[0, k:k + 1, :] for k in range(4)]

    def scan(i, carry):
        order = (i, jnp.where(i < nj_ctx, nj_ctx - 1 - i, nj - 1 - (i - nj_ctx)))
        new = []
        for d in range(2):
            rows = pl.ds(order[d], bsz, stride=nj)
            z_re, z_im = carry[2 * d], carry[2 * d + 1]
            a_re, a_im = lam[2 * d], lam[2 * d + 1]
            z_scr[2 * d, rows, :] = z_re
            z_scr[2 * d + 1, rows, :] = z_im
            w_re = w_scr[2 * d, rows, :]
            w_im = w_scr[2 * d + 1, rows, :]
            new.append(a_re * z_re - a_im * z_im + w_re)
            new.append(a_re * z_im + a_im * z_re + w_im)
        return tuple(new)

    z0 = jnp.zeros((bsz, LANE), F32)
    lax.fori_loop(0, nj, scan, (z0, z0, z0, z0))

    tmat = t_ref[0]
    pmat = p_ref[0]
    dsk = dsk_ref[0]

    def emit(b, _):
        r0 = pl.multiple_of(b * nj, 8)
        u = u_ref[0, b]
        zp = jnp.concatenate([z_scr[k, pl.ds(r0, nj), :].astype(BF16) for k in range(4)], axis=1)
        y = _dot(u, tmat) + _dot(zp, pmat) + dsk * u.astype(F32)
        y = 0.5 * y * (1.0 + jnp.tanh(math.sqrt(2.0 / math.pi) * (y + 0.044715 * (y * y * y))))
        o_ref[0, b] = y.astype(BF16)
        return 0

    lax.fori_loop(0, bsz, emit, 0)


def _s5(z, p, n_ctx):
    bsz, s, _ = z.shape
    nj = s // S5_CHUNK
    npair = S5_GROUPS // 2
    width = 2 * S5_CHUNK * S5_GROUP
    packed_spec = pl.BlockSpec((4, 1, nj, width), lambda b, q: (q, b, 0, 0))
    place_spec = pl.BlockSpec((4, 4, LANE, LANE), lambda b, q: (0, 0, 0, 0))
    packed_shape = jax.ShapeDtypeStruct((npair, bsz, nj, width), BF16)
    u = pl.pallas_call(
        functools.partial(_s5_pack_kernel, nj=nj), name="s5_pack",
        grid=(bsz, 4),
        in_specs=[pl.BlockSpec((1, s, LANE), lambda b, q: (b, 0, CB_S5 + q)), place_spec],
        out_specs=packed_spec, out_shape=packed_shape,
        scratch_shapes=[pltpu.VMEM((s, LANE), F32)],
        compiler_params=_cparams(("parallel", "parallel")),
    )(z, p["place"])
    blk = lambda shape: pl.BlockSpec((1,) + shape, lambda g: (g,) + (0,) * len(shape))
    y = pl.pallas_call(
        functools.partial(_s5_kernel, n_ctx=n_ctx, s_len=s, bsz=bsz), name="s5_core",
        grid=(npair,),
        in_specs=[blk((bsz, nj, width)), blk((width, width)), blk((width, width)), blk((width, width)),
                  blk((8, LANE)), blk((1, width))],
        out_specs=blk((bsz, nj, width)), out_shape=packed_shape,
        scratch_shapes=[pltpu.VMEM((4, bsz * nj, LANE), F32), pltpu.VMEM((4, bsz * nj, LANE), F32)],
        compiler_params=_cparams(("parallel",)),
    )(u, p["t"], p["q"], p["p"], p["lam"], p["dsk"])
    return pl.pallas_call(
        functools.partial(_s5_unpack_kernel, nj=nj), name="s5_unpack",
        grid=(bsz, 4),
        in_specs=[packed_spec, place_spec],
        out_specs=pl.BlockSpec((1, s, LANE), lambda b, q: (b, 0, q)),
        out_shape=jax.ShapeDtypeStruct((bsz, s, S5_WIDTH), BF16),
        scratch_shapes=[pltpu.VMEM((s, LANE), F32)],
        compiler_params=_cparams(("parallel", "parallel")),
    )(y, p["place"])


def _merge_kernel(a_ref, b_ref, c_ref, d_ref, g0_ref, g1_ref, g2_ref, g3_ref, x_ref, mod_ref,
                  pa_ref, pb_ref, pc_ref, gw_ref, gb_ref, wo_ref, nf_ref, rw_ref, rb_ref,
                  xo_ref, h_ref, lg_ref):
    sig = lambda r: jax.nn.sigmoid(r[0].astype(F32))
    glu = _dot(d_ref[0], gw_ref[...]) + gb_ref[...]
    yd = glu[:, :D_MODEL] * jax.nn.sigmoid(glu[:, D_MODEL:])
    y = sig(g0_ref) * _dot(a_ref[0], pa_ref[...]) + sig(g1_ref) * _dot(b_ref[0], pb_ref[...])
    y = y + sig(g2_ref) * _dot(c_ref[0], pc_ref[...]) + sig(g3_ref) * yd
    y = _dot(y.astype(BF16), wo_ref[...])
    x = x_ref[0] + mod_ref[0, 0, 2:3, :] * y
    xo_ref[0] = x
    h = x * lax.rsqrt(jnp.mean(x * x, axis=-1, keepdims=True) + EPS) * nf_ref[...]
    h = h * (1.0 + mod_ref[0, 0, 4:5, :]) + mod_ref[0, 0, 3:4, :]
    h_ref[0] = h.astype(BF16)
    hi, lo = _hi_lo(h)
    lg_ref[0] = _dot(hi, rw_ref[0]) + _dot(lo, rw_ref[0]) + _dot(hi, rw_ref[1]) + rb_ref[...]


def _merge(a, b, c, d, z, x, mod, w, n_ctx, tr):
    bsz, s, dm = x.shape
    nct = n_ctx // tr
    tile = lambda width: pl.BlockSpec((1, tr, width), lambda bb, j: (bb, j, 0))
    gate = lambda k: pl.BlockSpec((1, tr, dm), lambda bb, j: (bb, j, CB_GATE * LANE // dm + k))
    const = lambda shape: pl.BlockSpec(shape, lambda bb, j: (0,) * len(shape))
    return pl.pallas_call(
        _merge_kernel, name="merge_out",
        grid=(bsz, s // tr),
        in_specs=[tile(A_WIDTH), tile(B_INNER), tile(C_WIDTH), tile(S5_WIDTH),
                  gate(0), gate(1), gate(2), gate(3), tile(dm),
                  pl.BlockSpec((1, 1, 6, dm), lambda bb, j: (bb, jnp.minimum(j // nct, 1), 0, 0)),
                  const((A_WIDTH, dm)), const((B_INNER, dm)), const((C_WIDTH, dm)),
                  const((S5_WIDTH, 2 * dm)), const((1, 2 * dm)), const((dm, dm)), const((1, dm)),
                  const((2, dm, LANE)), const((1, LANE))],
        out_specs=[tile(dm), tile(dm), tile(LANE)],
        out_shape=[jax.ShapeDtypeStruct((bsz, s, dm), F32), jax.ShapeDtypeStruct((bsz, s, dm), BF16),
                   jax.ShapeDtypeStruct((bsz, s, LANE), F32)],
        compiler_params=_cparams(("parallel", "parallel")),
    )(a, b, c, d, z, z, z, z, x, mod, w["proj_a"], w["proj_b"], w["proj_c"], w["glu_w"], w["glu_b"],
      w["w_out"], w["norm_ffn"], w["router_w"], w["router_b"])


def _moe_kernel(be_ref, na_ref, hs_ref, gate_ref, w1_ref, b1g_ref, b1l_ref, w2_ref, b2_ref, perm_ref, o_ref,
                w1g_scr, w1l_scr, w2_scr):
    i = pl.program_id(0)
    active = i < na_ref[0]
    fresh = jnp.logical_and(active, jnp.logical_or(i == 0, be_ref[i] != be_ref[jnp.maximum(i - 1, 0)]))

    @pl.when(fresh)
    def _():
        dm, de2 = w1_ref.shape[2], w1_ref.shape[3]
        rows = 2 * LANE
        for r in range(dm // rows):
            w = w1_ref[0, 0, r * rows:(r + 1) * rows, :].astype(BF16)
            for c in range(de2 // (2 * LANE)):
                p = _dot(w[:, c * 2 * LANE:(c + 1) * 2 * LANE], perm_ref[...])
                w1g_scr[r * rows:(r + 1) * rows, c * LANE:(c + 1) * LANE] = p[:, :LANE].astype(BF16)
                w1l_scr[r * rows:(r + 1) * rows, c * LANE:(c + 1) * LANE] = p[:, LANE:].astype(BF16)
        w2_scr[...] = w2_ref[0, 0].astype(BF16)

    @pl.when(active)
    def _():
        hs = hs_ref[...]
        glu = jnp.minimum(_dot(hs, w1g_scr[...]) + b1g_ref[0], SWIGLU_LIMIT)
        lin = jnp.clip(_dot(hs, w1l_scr[...]) + b1l_ref[0], -SWIGLU_LIMIT, SWIGLU_LIMIT)
        act = glu * jax.nn.sigmoid(SWIGLU_ALPHA * glu) * (lin + 1.0)
        y = _dot(act.astype(BF16), w2_scr[...]) + b2_ref[0]
        o_ref[...] = (y * gate_ref[...]).astype(o_ref.dtype)

    @pl.when(jnp.logical_not(active))
    def _():
        o_ref[...] = jnp.zeros_like(o_ref)


def _moe_ffn(hs, gate_rows, block_exp, n_active, w1_all, w2_all, layer, b1g, b1l, b2):
    n_rows, dm = hs.shape
    tm = MOE_TM
    n_blocks = n_rows // tm
    de2 = w1_all.shape[-1]
    de = de2 // 2
    src = np.arange(2 * LANE)[:, None]
    dst = np.arange(2 * LANE)[None, :]
    perm = jnp.asarray(np.where(dst < LANE, src == 2 * dst, src == 2 * (dst - LANE) + 1), BF16)
    bspec = lambda shape: pl.BlockSpec((1,) + shape, lambda i, be, na: (be[i], 0, 0))
    wspec = lambda shape: pl.BlockSpec((1, 1) + shape, lambda i, be, na: (layer, be[i], 0, 0))
    grid_spec = pltpu.PrefetchScalarGridSpec(
        num_scalar_prefetch=2,
        grid=(n_blocks,),
        in_specs=[pl.BlockSpec((tm, dm), lambda i, be, na: (i, 0)),
                  pl.BlockSpec((tm, 1), lambda i, be, na: (i, 0)),
                  wspec((dm, de2)), bspec((1, de)), bspec((1, de)), wspec((de, dm)), bspec((1, dm)),
                  pl.BlockSpec((2 * LANE, 2 * LANE), lambda i, be, na: (0, 0))],
        out_specs=pl.BlockSpec((tm, dm), lambda i, be, na: (i, 0)),
        scratch_shapes=[pltpu.VMEM((dm, de), BF16), pltpu.VMEM((dm, de), BF16), pltpu.VMEM((de, dm), BF16)],
    )
    return pl.pallas_call(
        _moe_kernel, grid_spec=grid_spec, name="moe_ffn",
        out_shape=jax.ShapeDtypeStruct((n_rows, dm), BF16),
        compiler_params=_cparams(("arbitrary",)),
    )(block_exp, n_active, hs, gate_rows, w1_all, b1g, b1l, w2_all, b2, perm)


def _route(logits, n_tok):
    tm = MOE_TM
    top_val, top_idx = lax.top_k(logits, TOP_K)
    gate = jax.nn.softmax(top_val, axis=-1).reshape(-1)
    n_assign = n_tok * TOP_K
    e_flat = top_idx.reshape(-1).astype(jnp.int32)
    ar = jnp.arange(n_assign, dtype=jnp.int32)
    e_sorted, order = lax.sort((e_flat, ar), num_keys=1, is_stable=True)
    _, inv = lax.sort((order, ar), num_keys=1)
    counts = jnp.sum(jax.nn.one_hot(e_flat, N_EXPERTS, dtype=jnp.int32), axis=0)
    padded = (counts + tm - 1) // tm * tm
    start = jnp.cumsum(counts) - counts
    pend = jnp.cumsum(padded)
    pstart = pend - padded
    n_rows = -(-n_assign // tm) * tm + N_EXPERTS * tm
    n_blocks = n_rows // tm
    block_start = jnp.arange(n_blocks, dtype=jnp.int32) * tm
    block_exp = jnp.minimum(jnp.sum((pend[None, :] <= block_start[:, None]).astype(jnp.int32), axis=1), N_EXPERTS - 1)
    n_active = (pend[-1] // tm).astype(jnp.int32).reshape(1)
    row = jnp.arange(n_rows, dtype=jnp.int32).reshape(n_blocks, tm)
    within = row - pstart[block_exp][:, None]
    valid = jnp.logical_and(within < counts[block_exp][:, None], row < pend[-1]).reshape(-1)
    idx = jnp.clip(start[block_exp][:, None] + within, 0, n_assign - 1).reshape(-1)
    src = jnp.take(order, idx, mode="clip")
    buf_tok = jnp.where(valid, src // TOP_K, 0)
    buf_gate = jnp.where(valid, jnp.take(gate, src, mode="clip"), 0.0)
    dest = (pstart[e_sorted] + jnp.arange(n_assign, dtype=jnp.int32) - start[e_sorted])[inv]
    return buf_tok, buf_gate.reshape(n_rows, 1), block_exp, n_active, dest.reshape(n_tok, TOP_K)


def _s5_tables(lam_re, lam_im, log_step, b_re, b_im, c_re, c_im, d_skip):
    lc = S5_CHUNK
    lam = lax.complex(lam_re.astype(F32), lam_im.astype(F32))
    step = jnp.exp(log_step.astype(F32))[..., None]
    lam_dt = lam * step
    lam_bar = jnp.exp(lam_dt)
    b_cplx = lax.complex(b_re.astype(F32), b_im.astype(F32))
    c_cplx = lax.complex(c_re.astype(F32), c_im.astype(F32))
    b_bar = ((lam_bar - 1.0) / lam)[..., None] * b_cplx[None]
    kk = jnp.arange(lc + 1, dtype=F32)
    pw = jnp.exp(lam_dt[..., None] * kk)
    kern = jnp.einsum("gcp,dgpk,dgpe->dgkce", c_cplx, pw[..., :lc], b_bar).real
    t_idx = jnp.arange(lc)
    lag = t_idx[:, None] - t_idx[None, :]
    kf = kern[0][:, jnp.clip(lag, 0, lc - 1)]
    kb = kern[1][:, jnp.clip(-lag, 0, lc - 1)]
    sel_f = (lag >= 0)[None, :, :, None, None]
    sel_b = (lag <= 0)[None, :, :, None, None]
    toep = jnp.where(sel_f, kf, 0.0) + jnp.where(sel_b, kb, 0.0)
    tmat = toep.transpose(0, 2, 4, 1, 3).reshape(S5_GROUPS, lc * S5_GROUP, lc * S5_GROUP)
    qf = jnp.einsum("gps,gpe->gsep", pw[0][..., :lc][..., ::-1], b_bar[0]).reshape(S5_GROUPS, lc * S5_GROUP, S5_STATE)
    qb = jnp.einsum("gps,gpe->gsep", pw[1][..., :lc], b_bar[1]).reshape(S5_GROUPS, lc * S5_GROUP, S5_STATE)
    pf = jnp.einsum("gcp,gpt->gptc", c_cplx, pw[0][..., 1:]).reshape(S5_GROUPS, S5_STATE, lc * S5_GROUP)
    pb = jnp.einsum("gcp,gpt->gptc", c_cplx, pw[1][..., 1:][..., ::-1]).reshape(S5_GROUPS, S5_STATE, lc * S5_GROUP)
    lam_chunk = pw[..., lc]
    npair = S5_GROUPS // 2
    gw = lc * S5_GROUP
    pair = lambda a: a.reshape((npair, 2) + a.shape[1:])
    tm2, qf2, qb2, pf2, pb2 = pair(tmat), pair(qf), pair(qb), pair(pf), pair(pb)
    t_pair = jnp.zeros((npair, 2 * gw, 2 * gw), F32)
    q_pair = jnp.zeros((npair, 2 * gw, 4 * LANE), F32)
    p_pair = jnp.zeros((npair, 4 * LANE, 2 * gw), F32)
    for k in range(2):
        rows = slice(k * gw, (k + 1) * gw)
        t_pair = t_pair.at[:, rows, rows].set(tm2[:, k])
        for d, (qq, pp) in enumerate(((qf2, pf2), (qb2, pb2))):
            c_re_ = slice(2 * d * LANE + k * S5_STATE, 2 * d * LANE + (k + 1) * S5_STATE)
            c_im_ = slice((2 * d + 1) * LANE + k * S5_STATE, (2 * d + 1) * LANE + (k + 1) * S5_STATE)
            q_pair = q_pair.at[:, rows, c_re_].set(qq[:, k].real)
            q_pair = q_pair.at[:, rows, c_im_].set(qq[:, k].imag)
            p_pair = p_pair.at[:, c_re_, rows].set(pp[:, k].real)
            p_pair = p_pair.at[:, c_im_, rows].set(-pp[:, k].imag)
    lam_c = lam_chunk.reshape(2, npair, 2 * S5_STATE)
    lam_rows = jnp.stack([lam_c[0].real, lam_c[0].imag, lam_c[1].real, lam_c[1].imag], axis=1)
    lam_rows = jnp.concatenate([lam_rows, jnp.zeros((npair, 4, LANE), F32)], axis=1)
    dsk = jnp.tile(d_skip.astype(F32).reshape(S5_GROUPS, 1, S5_GROUP), (1, lc, 1)).reshape(npair, 1, 2 * gw)
    s_i, k_i, c_i = np.meshgrid(np.arange(lc), np.arange(2), np.arange(S5_GROUP), indexing="ij")
    idx = (k_i * gw + s_i * S5_GROUP + c_i).reshape(-1)
    t_pair = t_pair[:, idx][:, :, idx]
    q_pair = q_pair[:, idx]
    p_pair = p_pair[:, :, idx]
    dsk = dsk[:, :, idx]
    return {"t": t_pair.astype(BF16), "q": q_pair.astype(BF16), "p": p_pair.astype(BF16), "lam": lam_rows, "dsk": dsk}


def _s5_place():
    lanes = np.arange(LANE)
    place = np.zeros((4, 4, LANE, LANE), np.float32)
    for a in range(4):
        for i in range(4):
            place[a, i] = (lanes[:, None] // 32 == a) & (lanes[None, :] // 32 == i) & (lanes[:, None] % 32 == lanes[None, :] % 32)
    return jnp.asarray(place, BF16)


def _na_bias_table(rpb):
    cols = jnp.arange(GRID_W)
    col_start = jnp.clip(cols - NA_KW // 2, 0, GRID_W - NA_KW)
    in_win = (cols[None, :] >= col_start[:, None]) & (cols[None, :] < col_start[:, None] + NA_KW)
    dc = jnp.clip(cols[None, :] - cols[:, None], -(NA_KW - 1), NA_KW - 1) + NA_KW - 1
    e = jnp.arange(2 * NA_KH + 2)
    tabs = []
    for i in range(2):
        dr = e - 1 - i
        ok = (dr >= 0) & (dr <= 2 * NA_KH - 2)
        b = rpb.astype(F32)[:, jnp.clip(dr, 0, 2 * NA_KH - 2)][:, :, dc]
        b = jnp.where(ok[None, :, None, None] & in_win[None, None], b, NEG_BIG)
        tabs.append(b.transpose(0, 1, 3, 2))
    return jnp.concatenate(tabs, axis=-1)


def _rope_tables(n_ctx, t_len):
    pos = jnp.arange(t_len)
    row, col = pos // GRID_W, pos % GRID_W
    half = B_STATE // 2
    inv_freq = ROPE_THETA ** (-jnp.arange(0, half, 2, dtype=F32) / half)
    ang_r = row.astype(F32)[:, None] * inv_freq[None, :]
    ang_c = col.astype(F32)[:, None] * inv_freq[None, :]
    cos = jnp.concatenate([jnp.cos(ang_r)] * 2 + [jnp.cos(ang_c)] * 2, axis=-1)
    sin = jnp.concatenate([-jnp.sin(ang_r), jnp.sin(ang_r), -jnp.sin(ang_c), jnp.sin(ang_c)], axis=-1)
    cos = jnp.concatenate([jnp.ones((n_ctx, B_STATE), F32), cos], axis=0)
    sin = jnp.concatenate([jnp.zeros((n_ctx, B_STATE), F32), sin], axis=0)
    return cos, sin


def _ssd_consts(n_ctx, t_len):
    cos, sin = _rope_tables(n_ctx, t_len)
    lanes = np.arange(LANE)[:, None]
    cols = np.arange(B_INNER)[None, :]
    expand = np.stack([(lanes == d * B_HEADS + cols // B_HEADDIM) for d in range(2)])
    sel = np.stack([np.broadcast_to(np.arange(LANE)[:, None] == j, (LANE, LANE)) for j in range(2 * B_HEADS)])
    return {"cos": cos, "sin": sin, "expand": jnp.asarray(expand, BF16), "sel": jnp.asarray(sel, BF16)}


def _ssd_params(conv_w, conv_b, a_log, dt_bias, d_skip, norm_g):
    a = -jnp.exp(a_log.astype(F32)).reshape(-1)
    pad = lambda v: jnp.concatenate([v, jnp.zeros((LANE - v.shape[0],), F32)])
    vec = jnp.concatenate([pad(dt_bias.astype(F32).reshape(-1))[None], pad(a)[None], jnp.zeros((6, LANE), F32)], axis=0)
    cw = jnp.concatenate([conv_w.astype(F32), jnp.zeros((8 - B_CONV, B_CONV_DIM), F32)], axis=0)
    return {"conv_w": cw, "conv_b": conv_b.astype(F32).reshape(1, -1), "vec": vec,
            "dskip": jnp.repeat(d_skip.astype(F32), B_HEADDIM).reshape(1, -1),
            "norm": norm_g.astype(F32).reshape(1, -1)}


def _gla_params(lb, norm_g):
    lbh = lb.astype(F32).reshape(A_HEADS, A_DK)
    rows = jnp.stack([lbh, jnp.log(lbh), jnp.log1p(-lbh), 1.0 - lbh,
                      jnp.broadcast_to(norm_g.astype(F32), (A_HEADS, A_DK))], axis=1)
    return jnp.concatenate([rows, jnp.zeros((A_HEADS, 3, A_DK), F32)], axis=1)


def kernel(x, c, ctx, c_ctx, ada_w, ada_b, norm_mix, norm_ffn, w_in, hgrn_lb_logits, hgrn_norm, mamba_conv_w, mamba_conv_b, mamba_a_log, mamba_dt_bias, mamba_d, mamba_norm, na_rpb, s5_lam_re, s5_lam_im, s5_log_step, s5_b_re, s5_b_im, s5_c_re, s5_c_im, s5_d, s5_glu_w, s5_glu_b, proj_a, proj_b, proj_c, w_out, router_w, router_b, exp_w1, exp_b1, exp_w2, exp_b2, final_norm):
    bsz, t_len, dm = x.shape
    n_ctx = ctx.shape[1]
    s = n_ctx + t_len
    depth = w_in.shape[0]
    tr = math.gcd(n_ctx, 256)
    n_tok = bsz * s
    hp = lax.Precision.HIGHEST

    lb_all = jnp.cumsum(jax.nn.softmax(hgrn_lb_logits.astype(F32), axis=0), axis=0)
    lb_all = lb_all - lb_all[0:1]
    cond = jnp.concatenate([jax.nn.silu(c), jax.nn.silu(c_ctx)[None]], axis=0)
    m_all = (jnp.einsum("bd,ldk->lbk", cond, ada_w, precision=hp) + ada_b[:, None]).reshape(depth, bsz + 1, 6, dm)
    mod_all = jnp.stack([jnp.broadcast_to(m_all[:, bsz:bsz + 1], (depth, bsz, 6, dm)), m_all[:, :bsz]], axis=2)
    d_in = w_in.shape[-1]
    dt_lo = 4096
    w_re_all = jnp.concatenate([w_in[:, :, :dt_lo], w_in[:, :, dt_lo + 2 * B_HEADS:], w_in[:, :, dt_lo:dt_lo + 2 * B_HEADS],
                                jnp.zeros((depth, dm, Z_WIDTH - d_in), w_in.dtype)], axis=2).astype(BF16)
    gla_all = jax.vmap(_gla_params)(lb_all, hgrn_norm)
    ssd_all = jax.vmap(_ssd_params)(mamba_conv_w, mamba_conv_b, mamba_a_log, mamba_dt_bias, mamba_d, mamba_norm)
    ssd_const = _ssd_consts(n_ctx, t_len)
    na_all = jax.vmap(_na_bias_table)(na_rpb)
    s5_all = jax.vmap(_s5_tables)(s5_lam_re, s5_lam_im, s5_log_step, s5_b_re, s5_b_im, s5_c_re, s5_c_im, s5_d)
    s5_place = _s5_place()
    rw = jnp.concatenate([router_w.astype(F32), jnp.zeros((depth, dm, LANE - N_EXPERTS), F32)], axis=2)
    rw_hi = rw.astype(BF16)
    rw_all = jnp.stack([rw_hi, (rw - rw_hi.astype(F32)).astype(BF16)], axis=1)
    rb_all = jnp.concatenate([router_b.astype(F32), jnp.zeros((depth, LANE - N_EXPERTS), F32)], axis=1)
    wts_all = {"proj_a": proj_a.astype(BF16), "proj_b": proj_b.astype(BF16), "proj_c": proj_c.astype(BF16),
               "glu_w": s5_glu_w.astype(BF16), "glu_b": s5_glu_b.astype(F32)[:, None, :],
               "w_out": w_out.astype(BF16), "norm_ffn": norm_ffn.astype(F32)[:, None, :],
               "router_w": rw_all, "router_b": rb_all[:, None, :]}
    b1g_all = exp_b1[:, :, None, 0::2].astype(F32)
    b1l_all = exp_b1[:, :, None, 1::2].astype(F32)
    b2_all = exp_b2[:, :, None, :].astype(F32)
    layer = lambda tree, l: jax.tree.map(lambda a: a[l], tree)

    xa = jnp.concatenate([ctx, x], axis=1)
    ymoe = None
    mod_prev = None
    for l in range(depth):
        mod = mod_all[l]
        if ymoe is None:
            (h,) = _prep(xa, mod, norm_mix[l], n_ctx, tr)
        else:
            xa, h = _prep(xa, mod, norm_mix[l], n_ctx, tr, ymoe=ymoe, mod_prev=mod_prev)
        z = _in_proj(h.reshape(n_tok, dm), w_re_all[l]).reshape(bsz, s, Z_WIDTH)

        a_out = _gla(z, gla_all[l], n_ctx)
        b_out = _ssd(z, {**layer(ssd_all, l), **ssd_const}, n_ctx)
        c_out = _na(z, na_all[l], n_ctx)
        d_out = _s5(z, {**layer(s5_all, l), "place": s5_place}, n_ctx)
        xa, h2, logits = _merge(a_out, b_out, c_out, d_out, z, xa, mod, layer(wts_all, l), n_ctx, tr)

        buf_tok, buf_gate, block_exp, n_active, dest = _route(logits.reshape(n_tok, LANE)[:, :N_EXPERTS], n_tok)
        hs = jnp.take(h2.reshape(n_tok, dm), buf_tok, axis=0, mode="clip")
        yb = _moe_ffn(hs, buf_gate, block_exp, n_active, exp_w1, exp_w2, l, b1g_all[l], b1l_all[l], b2_all[l])
        ymoe = jnp.take(yb, dest.T, axis=0, mode="clip").reshape(TOP_K, bsz, s, dm)
        mod_prev = mod

    zero_mod = jnp.zeros_like(mod_prev)
    (out,) = _prep(xa, zero_mod, final_norm, n_ctx, tr, ymoe=ymoe, mod_prev=mod_prev, final=True)
    return out
```
